```python
import math
import jax, jax.numpy as jnp
from jax import lax
import numpy as np

D_MODEL = 1024
BATCH = 4
SEQ = 4096
DEPTH = 2

CHUNK = 64
N_A_LAYERS = DEPTH // 2
N_B_LAYERS = DEPTH - N_A_LAYERS
N_DENSE_FFN = (DEPTH + 1) // 2
N_MOE_FFN = DEPTH // 2
DEEPNORM_ALPHA = (2.0 * DEPTH) ** 0.25
DEEPNORM_BETA = (8.0 * DEPTH) ** -0.25
LN_EPS = 1e-5

RET_HEADS = 4
RET_QK_DIM = D_MODEL // RET_HEADS
RET_V_WIDTH = 2 * D_MODEL
RET_V_DIM = RET_V_WIDTH // RET_HEADS
RET_IN_WIDTH = 2 * D_MODEL + 2 * RET_V_WIDTH
ROPE_BASE = 10000.0

SB_HEADS = 16
SB_HEAD_DIM = D_MODEL // SB_HEADS
SB_QBLOCK = 128

D_FF = 2816
N_EXPERTS = 8
TOP_K = 2
D_EXPERT = 3584

kernel_name = "retention_stickbreaking_yoco_moe_deepnorm"


def layer_norm(x, g, b):
    xf = x.astype(jnp.float32)
    mu = jnp.mean(xf, axis=-1, keepdims=True)
    var = jnp.mean(jnp.square(xf - mu), axis=-1, keepdims=True)
    y = (xf - mu) * lax.rsqrt(var + LN_EPS) * g.astype(jnp.float32) + b.astype(jnp.float32)
    return y.astype(x.dtype)


def rotary(x):
    s, d = x.shape[-2], x.shape[-1]
    inv_freq = 1.0 / (ROPE_BASE ** (jnp.arange(0, d, 2, dtype=jnp.float32) / d))
    ang = jnp.arange(s, dtype=jnp.float32)[:, None] * inv_freq[None, :]
    cos, sin = jnp.cos(ang).astype(x.dtype), jnp.sin(ang).astype(x.dtype)
    x1, x2 = x[..., : d // 2], x[..., d // 2:]
    return jnp.concatenate([x1 * cos - x2 * sin, x1 * sin + x2 * cos], axis=-1)


def retention_mixer(x, w_in, w_out):
    b, s, _ = x.shape
    nc = s // CHUNK
    dt = x.dtype
    proj = x @ w_in
    q, k, v, g = jnp.split(proj, [D_MODEL, 2 * D_MODEL, 2 * D_MODEL + RET_V_WIDTH], axis=-1)
    q = q.reshape(b, s, RET_HEADS, RET_QK_DIM).transpose(0, 2, 1, 3)
    k = k.reshape(b, s, RET_HEADS, RET_QK_DIM).transpose(0, 2, 1, 3) * (RET_QK_DIM ** -0.5)
    v = v.reshape(b, s, RET_HEADS, RET_V_DIM).transpose(0, 2, 1, 3)
    q, k = rotary(q), rotary(k)
    log_g = jnp.log(1.0 - 2.0 ** (-5.0 - jnp.arange(RET_HEADS, dtype=jnp.float32)))
    idx = jnp.arange(CHUNK, dtype=jnp.float32)
    intra_decay = jnp.exp(log_g[:, None, None] * jnp.abs(idx[:, None] - idx[None, :])).astype(dt)
    q_decay = jnp.exp(log_g[:, None] * (idx + 1.0))[..., None].astype(dt)
    k_decay = jnp.exp(log_g[:, None] * (CHUNK - 1.0 - idx))[..., None].astype(dt)
    chunk_decay = jnp.exp(log_g * CHUNK)[:, None, None].astype(dt)

    def to_chunks(t):
        return t.reshape(b, RET_HEADS, nc, CHUNK, t.shape[-1]).transpose(2, 0, 1, 3, 4)

    def step(state, inp):
        qi, ki, vi = inp
        scores = jnp.einsum('bhnd,bhmd->bhnm', qi, ki) * intra_decay
        out = (jnp.einsum('bhnm,bhme->bhne', scores, vi)
               + jnp.einsum('bhnd,bhde->bhne', qi * q_decay, state))
        state = state * chunk_decay + jnp.einsum('bhmd,bhme->bhde', ki * k_decay, vi)
        return state, out

    state0 = jnp.zeros((b, RET_HEADS, RET_QK_DIM, RET_V_DIM), dt)
    _, oc = lax.scan(step, state0, (to_chunks(q), to_chunks(k), to_chunks(v)))
    o = oc.transpose(1, 0, 3, 2, 4).reshape(b, s, RET_HEADS, RET_V_DIM)
    of = o.astype(jnp.float32)
    mu = jnp.mean(of, axis=-1, keepdims=True)
    var = jnp.mean(jnp.square(of - mu), axis=-1, keepdims=True)
    o = ((of - mu) * lax.rsqrt(var + LN_EPS)).astype(dt).reshape(b, s, RET_V_WIDTH)
    return (jax.nn.silu(g) * o) @ w_out


def shared_kv(x, w_kv):
    b, s, _ = x.shape
    kv = x @ w_kv
    k, v = jnp.split(kv, 2, axis=-1)
    k = k.reshape(b, s, SB_HEADS, SB_HEAD_DIM).transpose(0, 2, 1, 3)
    v = v.reshape(b, s, SB_HEADS, SB_HEAD_DIM).transpose(0, 2, 1, 3)
    return k, v


def stick_breaking_mixer(x, k, v, w_q, w_out):
    b, s, _ = x.shape
    scale = SB_HEAD_DIM ** -0.5
    q = (x @ w_q).reshape(b, s, SB_HEADS, SB_HEAD_DIM).transpose(0, 2, 1, 3)
    outs = []
    for start in range(0, s, SB_QBLOCK):
        end = start + SB_QBLOCK
        qb = q[:, :, start:end]
        kb, vb = k[:, :, :end], v[:, :, :end]
        z = jnp.einsum('bhtd,bhsd->bhts', qb, kb).astype(jnp.float32) * scale
        causal = jnp.arange(end)[None, :] < jnp.arange(start, end)[:, None]
        log_keep = jnp.where(causal, jax.nn.log_sigmoid(-z), 0.0)
        between = lax.cumsum(log_keep, axis=3, reverse=True) - log_keep
        w = jnp.where(causal, jnp.exp(jax.nn.log_sigmoid(z) + between), 0.0)
        outs.append(jnp.einsum('bhts,bhsd->bhtd', w.astype(vb.dtype), vb))
    o = jnp.concatenate(outs, axis=2).transpose(0, 2, 1, 3).reshape(b, s, D_MODEL)
    return o @ w_out


def swiglu(x, w_in, w_out):
    gate, up = jnp.split(x @ w_in, 2, axis=-1)
    return (jax.nn.silu(gate) * up) @ w_out


def moe_swiglu(x, w_router, w_in, w_out):
    b, s, d = x.shape
    xt = x.reshape(b * s, d)
    logits = (xt @ w_router).astype(jnp.float32)
    top_val, top_idx = lax.top_k(logits, TOP_K)
    top_w = jax.nn.softmax(top_val, axis=-1)
    gates = jnp.sum(jax.nn.one_hot(top_idx, N_EXPERTS, dtype=jnp.float32) * top_w[..., None], axis=1)
    gates = gates.astype(x.dtype)
    y = jnp.zeros_like(xt)
    for e in range(N_EXPERTS):
        y = y + gates[:, e:e + 1] * swiglu(xt, w_in[e], w_out[e])
    return y.reshape(b, s, d)


def setup_inputs(seed: int = 0) -> dict:
    key = jax.random.key(seed)
    ks = jax.random.split(key, 20)
    f32 = jnp.float32
    beta = DEEPNORM_BETA

    def nrm(k, shape, fan_in, gain=1.0):
        return jax.random.normal(k, shape, f32) * (fan_in ** -0.5) * gain

    x = jax.random.normal(ks[0], (BATCH, SEQ, D_MODEL), f32)
    ret_w_in = jnp.concatenate([
        nrm(ks[1], (N_A_LAYERS, D_MODEL, 2 * D_MODEL), D_MODEL),
        nrm(ks[2], (N_A_LAYERS, D_MODEL, RET_V_WIDTH), D_MODEL, beta),
        nrm(ks[3], (N_A_LAYERS, D_MODEL, RET_V_WIDTH), D_MODEL),
    ], axis=-1)
    ret_w_out = nrm(ks[4], (N_A_LAYERS, RET_V_WIDTH, D_MODEL), RET_V_WIDTH, beta)
    sb_w_kv = jnp.concatenate([
        nrm(ks[5], (D_MODEL, D_MODEL), D_MODEL),
        nrm(ks[6], (D_MODEL, D_MODEL), D_MODEL, beta),
    ], axis=-1)
    sb_w_q = nrm(ks[7], (N_B_LAYERS, D_MODEL, D_MODEL), D_MODEL)
    sb_w_out = nrm(ks[8], (N_B_LAYERS, D_MODEL, D_MODEL), D_MODEL, beta)
    ln_mix_g = 1.0 + 0.02 * jax.random.normal(ks[9], (DEPTH, D_MODEL), f32)
    ln_mix_b = 0.02 * jax.random.normal(ks[10], (DEPTH, D_MODEL), f32)
    ln_ffn_g = 1.0 + 0.02 * jax.random.normal(ks[11], (DEPTH, D_MODEL), f32)
    ln_ffn_b = 0.02 * jax.random.normal(ks[12], (DEPTH, D_MODEL), f32)
    ffn_w_in = nrm(ks[13], (N_DENSE_FFN, D_MODEL, 2 * D_FF), D_MODEL, beta)
    ffn_w_out = nrm(ks[14], (N_DENSE_FFN, D_FF, D_MODEL), D_FF, beta)
    moe_w_router = nrm(ks[15], (N_MOE_FFN, D_MODEL, N_EXPERTS), D_MODEL)
    moe_w_in = nrm(ks[16], (N_MOE_FFN, N_EXPERTS, D_MODEL, 2 * D_EXPERT), D_MODEL, beta)
    moe_w_out = nrm(ks[17], (N_MOE_FFN, N_EXPERTS, D_EXPERT, D_MODEL), D_EXPERT, beta)
    return {"x": x, "ret_w_in": ret_w_in, "ret_w_out": ret_w_out, "sb_w_kv": sb_w_kv,
            "sb_w_q": sb_w_q, "sb_w_out": sb_w_out, "ln_mix_g": ln_mix_g, "ln_mix_b": ln_mix_b,
            "ln_ffn_g": ln_ffn_g, "ln_ffn_b": ln_ffn_b, "ffn_w_in": ffn_w_in, "ffn_w_out": ffn_w_out,
            "moe_w_router": moe_w_router, "moe_w_in": moe_w_in, "moe_w_out": moe_w_out}


def reference(x, ret_w_in, ret_w_out, sb_w_kv, sb_w_q, sb_w_out, ln_mix_g, ln_mix_b,
              ln_ffn_g, ln_ffn_b, ffn_w_in, ffn_w_out, moe_w_router, moe_w_in, moe_w_out):
    shared_k = None
    shared_v = None
    for layer in range(DEPTH):
        if layer < N_A_LAYERS:
            mix = retention_mixer(x, ret_w_in[layer], ret_w_out[layer])
        else:
            j = layer - N_A_LAYERS
            mix = stick_breaking_mixer(x, shared_k, shared_v, sb_w_q[j], sb_w_out[j])
        x = layer_norm(DEEPNORM_ALPHA * x + mix, ln_mix_g[layer], ln_mix_b[layer])
        if layer % 2 == 0:
            f = swiglu(x, ffn_w_in[layer // 2], ffn_w_out[layer // 2])
        else:
            f = moe_swiglu(x, moe_w_router[layer // 2], moe_w_in[layer // 2], moe_w_out[layer // 2])
        x = layer_norm(DEEPNORM_ALPHA * x + f, ln_ffn_g[layer], ln_ffn_b[layer])
        if layer == N_A_LAYERS - 1:
            shared_k, shared_v = shared_kv(x, sb_w_kv)
    return x
```

```python
import functools
import math

import jax
import jax.numpy as jnp
from jax import lax
from jax.experimental import pallas as pl
from jax.experimental.pallas import tpu as pltpu

D_MODEL = 1024
DEPTH = 2
CHUNK = 64
DEEPNORM_ALPHA = (2.0 * DEPTH) ** 0.25
LN_EPS = 1e-5

RET_HEADS = 4
RET_QK_DIM = D_MODEL // RET_HEADS
RET_V_WIDTH = 2 * D_MODEL
RET_V_DIM = RET_V_WIDTH // RET_HEADS
ROPE_BASE = 10000.0
RET_BLOCK = 256

SB_HEADS = 16
SB_HEAD_DIM = D_MODEL // SB_HEADS
SB_LANES = 128
SB_TILE = 256

D_FF = 2816
FFN_TF = 1408
N_EXPERTS = 8
D_EXPERT = 3584
MOE_TF = 512
MOE_TM = 512

VMEM_LIMIT = 56 * 1024 * 1024

BF16 = jnp.bfloat16
F32 = jnp.float32


def _params(*sem):
    return pltpu.CompilerParams(dimension_semantics=sem, vmem_limit_bytes=VMEM_LIMIT)


def _dot(a, b):
    return jnp.dot(a, b, preferred_element_type=F32)


def _dot_nt(a, b):
    return lax.dot_general(a, b, (((1,), (1,)), ((), ())), preferred_element_type=F32)


def _dot_tn(a, b):
    return lax.dot_general(a, b, (((0,), (0,)), ((), ())), preferred_element_type=F32)


def _layer_norm(y, g, b):
    mu = jnp.mean(y, axis=-1, keepdims=True)
    yc = y - mu
    var = jnp.mean(yc * yc, axis=-1, keepdims=True)
    return yc * lax.rsqrt(var + LN_EPS) * g + b


def _silu(x):
    return x * (1.0 / (1.0 + jnp.exp(-x)))


def _mm_kernel(a_ref, w_ref, o_ref):
    o_ref[...] = _dot(a_ref[...].astype(BF16), w_ref[...]).astype(o_ref.dtype)


def _mm(a, w, tm, tn, out_dtype):
    n, k = a.shape
    nout = w.shape[1]
    return pl.pallas_call(
        _mm_kernel,
        grid=(n // tm, nout // tn),
        in_specs=[pl.BlockSpec((tm, k), lambda i, j: (i, 0)),
                  pl.BlockSpec((k, tn), lambda i, j: (0, j))],
        out_specs=pl.BlockSpec((tm, tn), lambda i, j: (i, j)),
        out_shape=jax.ShapeDtypeStruct((n, nout), out_dtype),
        compiler_params=_params("parallel", "arbitrary"),
        name="proj_mm",
    )(a, w)


def _mm_res_ln_kernel(a_ref, w_ref, res_ref, g_ref, b_ref, o_ref):
    y = _dot(a_ref[...].astype(BF16), w_ref[...])
    y = DEEPNORM_ALPHA * res_ref[...] + y
    o_ref[...] = _layer_norm(y, g_ref[...], b_ref[...])


def _mm_res_ln(a, w, res, g, b, tm):
    n, k = a.shape
    d = w.shape[1]
    return pl.pallas_call(
        _mm_res_ln_kernel,
        grid=(n // tm,),
        in_specs=[pl.BlockSpec((tm, k), lambda i: (i, 0)),
                  pl.BlockSpec((k, d), lambda i: (0, 0)),
                  pl.BlockSpec((tm, d), lambda i: (i, 0)),
                  pl.BlockSpec((1, d), lambda i: (0, 0)),
                  pl.BlockSpec((1, d), lambda i: (0, 0))],
        out_specs=pl.BlockSpec((tm, d), lambda i: (i, 0)),
        out_shape=jax.ShapeDtypeStruct((n, d), F32),
        compiler_params=_params("parallel"),
        name="mm_res_ln",
    )(a, w, res, g, b)


def _rotary(x, cos, sin):
    half = x.shape[-1] // 2
    x1, x2 = x[:, :half], x[:, half:]
    return jnp.concatenate([x1 * cos - x2 * sin, x1 * sin + x2 * cos], axis=-1)


def _retention_kernel(q_ref, k_ref, v_ref, g_ref, cos_ref, sin_ref, dmat_ref,
                      qd_ref, kd_ref, cd_ref, o_ref, state_ref):
    @pl.when(pl.program_id(2) == 0)
    def _():
        state_ref[...] = jnp.zeros_like(state_ref)

    cos, sin = cos_ref[...], sin_ref[...]
    q = _rotary(q_ref[...].astype(F32), cos, sin)
    k = _rotary(k_ref[...].astype(F32) * (RET_QK_DIM ** -0.5), cos, sin)
    v = v_ref[...]
    qb = q.astype(BF16)
    state = state_ref[...]
    scores = _dot_nt(qb, k.astype(BF16)) * dmat_ref[0]
    o = _dot(scores.astype(BF16), v) + qd_ref[0] * _dot(qb, state.astype(BF16))
    state_ref[...] = state * cd_ref[0] + _dot_tn((k * kd_ref[0]).astype(BF16), v)
    mu = jnp.mean(o, axis=-1, keepdims=True)
    oc = o - mu
    var = jnp.mean(oc * oc, axis=-1, keepdims=True)
    on = oc * lax.rsqrt(var + LN_EPS)
    o_ref[...] = (_silu(g_ref[...].astype(F32)) * on).astype(o_ref.dtype)


def _retention_tables(seq):
    t = RET_BLOCK
    half = RET_QK_DIM // 2
    inv_freq = 1.0 / (ROPE_BASE ** (jnp.arange(0, RET_QK_DIM, 2, dtype=F32) / RET_QK_DIM))
    ang = jnp.arange(seq, dtype=F32)[:, None] * inv_freq[None, :]
    cos, sin = jnp.cos(ang), jnp.sin(ang)
    assert cos.shape == (seq, half)
    log_g = jnp.log(1.0 - 2.0 ** (-5.0 - jnp.arange(RET_HEADS, dtype=F32)))
    idx = jnp.arange(t, dtype=F32)
    diff = idx[:, None] - idx[None, :]
    chunk = jnp.arange(t) // CHUNK
    same = chunk[:, None] == chunk[None, :]
    earlier = chunk[None, :] < chunk[:, None]
    lg = log_g[:, None, None]
    dmat = jnp.where(same[None], jnp.exp(lg * jnp.abs(diff)[None]),
                     jnp.where(earlier[None], jnp.exp(lg * diff[None]), 0.0))
    qd = jnp.exp(log_g[:, None] * (idx + 1.0))[..., None]
    kd = jnp.exp(log_g[:, None] * (t - 1.0 - idx))[..., None]
    cd = jnp.broadcast_to(jnp.exp(log_g * t)[:, None, None], (RET_HEADS, 1, RET_V_DIM))
    return cos, sin, dmat, qd, kd, cd


def _retention(proj, batch, seq):
    n = proj.shape[0]
    t = RET_BLOCK
    nt = seq // t
    cos, sin, dmat, qd, kd, cd = _retention_tables(seq)
    qk_blocks = D_MODEL // RET_QK_DIM
    v_off = 2 * D_MODEL // RET_V_DIM
    g_off = v_off + RET_V_WIDTH // RET_V_DIM
    half = RET_QK_DIM // 2
    row = lambda b, h, i: b * nt + i
    return pl.pallas_call(
        _retention_kernel,
        grid=(batch, RET_HEADS, nt),
        in_specs=[
            pl.BlockSpec((t, RET_QK_DIM), lambda b, h, i: (row(b, h, i), h)),
            pl.BlockSpec((t, RET_QK_DIM), lambda b, h, i: (row(b, h, i), qk_blocks + h)),
            pl.BlockSpec((t, RET_V_DIM), lambda b, h, i: (row(b, h, i), v_off + h)),
            pl.BlockSpec((t, RET_V_DIM), lambda b, h, i: (row(b, h, i), g_off + h)),
            pl.BlockSpec((t, half), lambda b, h, i: (i, 0)),
            pl.BlockSpec((t, half), lambda b, h, i: (i, 0)),
            pl.BlockSpec((1, t, t), lambda b, h, i: (h, 0, 0)),
            pl.BlockSpec((1, t, 1), lambda b, h, i: (h, 0, 0)),
            pl.BlockSpec((1, t, 1), lambda b, h, i: (h, 0, 0)),
            pl.BlockSpec((1, 1, RET_V_DIM), lambda b, h, i: (h, 0, 0)),
        ],
        out_specs=pl.BlockSpec((t, RET_V_DIM), lambda b, h, i: (row(b, h, i), h)),
        out_shape=jax.ShapeDtypeStruct((n, RET_V_WIDTH), BF16),
        scratch_shapes=[pltpu.VMEM((RET_QK_DIM, RET_V_DIM), F32)],
        compiler_params=_params("parallel", "parallel", "arbitrary"),
        name="retention",
    )(proj, proj, proj, proj, cos, sin, dmat, qd, kd, cd)


def _ffn_kernel(x_ref, wg_ref, wu_ref, wo_ref, g_ref, b_ref, o_ref, acc_ref):
    j = pl.program_id(1)
    xb = x_ref[...].astype(BF16)
    act = _silu(_dot(xb, wg_ref[...])) * _dot(xb, wu_ref[...])
    part = _dot(act.astype(BF16), wo_ref[...])

    @pl.when(j == 0)
    def _():
        acc_ref[...] = part

    @pl.when(j > 0)
    def _():
        acc_ref[...] += part

    @pl.when(j == pl.num_programs(1) - 1)
    def _():
        y = DEEPNORM_ALPHA * x_ref[...] + acc_ref[...]
        o_ref[...] = _layer_norm(y, g_ref[...], b_ref[...])


def _ffn(x, w_in, w_out, g, b, tm):
    n, d = x.shape
    nf = D_FF // FFN_TF
    return pl.pallas_call(
        _ffn_kernel,
        grid=(n // tm, nf),
        in_specs=[pl.BlockSpec((tm, d), lambda i, j: (i, 0)),
                  pl.BlockSpec((d, FFN_TF), lambda i, j: (0, j)),
                  pl.BlockSpec((d, FFN_TF), lambda i, j: (0, nf + j)),
                  pl.BlockSpec((FFN_TF, d), lambda i, j: (j, 0)),
                  pl.BlockSpec((1, d), lambda i, j: (0, 0)),
                  pl.BlockSpec((1, d), lambda i, j: (0, 0))],
        out_specs=pl.BlockSpec((tm, d), lambda i, j: (i, 0)),
        out_shape=jax.ShapeDtypeStruct((n, d), F32),
        scratch_shapes=[pltpu.VMEM((tm, d), F32)],
        compiler_params=_params("parallel", "arbitrary"),
        name="ffn",
    )(x, w_in, w_in, w_out, g, b)


def _softplus(z):
    return jnp.maximum(z, 0.0) + jnp.log1p(jnp.exp(-jnp.abs(z)))


def _sb_kernel(q_ref, k_ref, v_ref, o_ref):
    i = pl.program_id(2)
    t = SB_TILE
    scale = SB_HEAD_DIM ** -0.5
    q2 = q_ref[...]
    lane = lax.broadcasted_iota(jnp.int32, (1, SB_LANES), 1)
    row = lax.broadcasted_iota(jnp.int32, (t, t), 0)
    col = lax.broadcasted_iota(jnp.int32, (t, t), 1)
    suffix = jnp.where(row > col, 1.0, 0.0).astype(BF16)
    causal = col < row

    def tile(j, qm, run, acc, diag):
        kt = k_ref[pl.ds(pl.multiple_of(j * t, t), t), :]
        vt = v_ref[pl.ds(pl.multiple_of(j * t, t), t), :]
        z = _dot_nt(qm, kt) * scale
        sp = _softplus(z)
        lk = -sp
        if diag:
            lk = jnp.where(causal, lk, 0.0)
        hi = lk.astype(BF16)
        lo = (lk - hi.astype(F32)).astype(BF16)
        between = _dot(hi, suffix) + _dot(lo, suffix) + run
        w = jnp.exp(z - sp + between)
        if diag:
            w = jnp.where(causal, w, 0.0)
        acc = acc + _dot(w.astype(BF16), vt)
        run = run + jnp.sum(lk, axis=1, keepdims=True)
        return run, acc

    outs = []
    for hh in range(SB_LANES // SB_HEAD_DIM):
        in_head = (lane >= hh * SB_HEAD_DIM) & (lane < (hh + 1) * SB_HEAD_DIM)
        qm = jnp.where(in_head, q2, jnp.zeros_like(q2))
        run0 = jnp.zeros((t, 1), F32)
        acc0 = jnp.zeros((t, SB_LANES), F32)
        run, acc = tile(i, qm, run0, acc0, True)

        def body(step, carry, qm=qm):
            run, acc = carry
            return tile(i - 1 - step, qm, run, acc, False)

        run, acc = lax.fori_loop(0, i, body, (run, acc))
        outs.append((in_head, acc))
    o = jnp.where(outs[0][0], outs[0][1], outs[1][1])
    o_ref[...] = o.astype(o_ref.dtype)


def _sb_attention(qkv, batch, seq):
    n = qkv.shape[0]
    t = SB_TILE
    nt = seq // t
    npair = D_MODEL // SB_LANES
    return pl.pallas_call(
        _sb_kernel,
        grid=(batch, npair, nt),
        in_specs=[
            pl.BlockSpec((t, SB_LANES), lambda b, p, i: (b * nt + i, p)),
            pl.BlockSpec((seq, SB_LANES), lambda b, p, i: (b, npair + p)),
            pl.BlockSpec((seq, SB_LANES), lambda b, p, i: (b, 2 * npair + p)),
        ],
        out_specs=pl.BlockSpec((t, SB_LANES), lambda b, p, i: (b * nt + i, p)),
        out_shape=jax.ShapeDtypeStruct((n, D_MODEL), BF16),
        compiler_params=_params("parallel", "parallel", "arbitrary"),
        name="sb_attention",
    )(qkv, qkv, qkv)


def _router_kernel(x_ref, wr_ref, o_ref):
    logits = lax.dot_general(wr_ref[...], x_ref[...], (((1,), (1,)), ((), ())),
                             precision=lax.Precision.HIGHEST,
                             preferred_element_type=F32)
    e_idx = lax.broadcasted_iota(jnp.int32, logits.shape, 0)
    m1 = jnp.max(logits, axis=0, keepdims=True)
    i1 = jnp.min(jnp.where(logits == m1, e_idx, N_EXPERTS), axis=0, keepdims=True)
    rest = jnp.where(e_idx == i1, -jnp.inf, logits)
    m2 = jnp.max(rest, axis=0, keepdims=True)
    i2 = jnp.min(jnp.where(rest == m2, e_idx, N_EXPERTS), axis=0, keepdims=True)
    e2 = jnp.exp(m2 - m1)
    den = 1.0 + e2
    g1 = 1.0 / den
    g2 = e2 / den
    zeros = jnp.zeros((N_EXPERTS - 4, logits.shape[1]), F32)
    o_ref[...] = jnp.concatenate([i1.astype(F32), i2.astype(F32), g1, g2, zeros], axis=0)


def _router(x, w_router_t, tm):
    n, d = x.shape
    return pl.pallas_call(
        _router_kernel,
        grid=(n // tm,),
        in_specs=[pl.BlockSpec((tm, d), lambda i: (i, 0)),
                  pl.BlockSpec((N_EXPERTS, d), lambda i: (0, 0))],
        out_specs=pl.BlockSpec((N_EXPERTS, tm), lambda i: (0, i)),
        out_shape=jax.ShapeDtypeStruct((N_EXPERTS, n), F32),
        compiler_params=_params("parallel"),
        name="router",
    )(x, w_router_t)


def _expert_kernel(te_ref, nu_ref, x_ref, wg_ref, wu_ref, wo_ref, o_ref, acc_ref):
    i = pl.program_id(0)
    j = pl.program_id(1)

    @pl.when(i < nu_ref[0])
    def _():
        xb = x_ref[...]
        act = _silu(_dot(xb, wg_ref[0])) * _dot(xb, wu_ref[0])
        part = _dot(act.astype(BF16), wo_ref[0])

        @pl.when(j == 0)
        def _():
            acc_ref[...] = part

        @pl.when(j > 0)
        def _():
            acc_ref[...] += part

    @pl.when(j == pl.num_programs(1) - 1)
    def _():
        o_ref[...] = acc_ref[...]


def _experts(xs, tile_expert, n_used, w_in, w_out):
    p, d = xs.shape
    tm = MOE_TM
    nf = D_EXPERT // MOE_TF
    grid_spec = pltpu.PrefetchScalarGridSpec(
        num_scalar_prefetch=2,
        grid=(p // tm, nf),
        in_specs=[
            pl.BlockSpec((tm, d), lambda i, j, te, nu: (i, 0)),
            pl.BlockSpec((1, d, MOE_TF), lambda i, j, te, nu: (te[i], 0, j)),
            pl.BlockSpec((1, d, MOE_TF), lambda i, j, te, nu: (te[i], 0, nf + j)),
            pl.BlockSpec((1, MOE_TF, d), lambda i, j, te, nu: (te[i], j, 0)),
        ],
        out_specs=pl.BlockSpec((tm, d), lambda i, j, te, nu: (i, 0)),
        scratch_shapes=[pltpu.VMEM((tm, d), F32)],
    )
    return pl.pallas_call(
        _expert_kernel,
        grid_spec=grid_spec,
        out_shape=jax.ShapeDtypeStruct((p, d), F32),
        compiler_params=_params("arbitrary", "arbitrary"),
        name="experts",
    )(tile_expert, n_used, xs, w_in, w_in, w_out)


def _combine_ln_kernel(x_ref, y1_ref, y2_ref, g1_ref, g2_ref, g_ref, b_ref, o_ref):
    y = (DEEPNORM_ALPHA * x_ref[...] + g1_ref[...] * y1_ref[...]) + g2_ref[...] * y2_ref[...]
    o_ref[...] = _layer_norm(y, g_ref[...], b_ref[...])


def _combine_ln(x, y1, y2, g1, g2, g, b, tm):
    n, d = x.shape
    rows = pl.BlockSpec((tm, d), lambda i: (i, 0))
    col = pl.BlockSpec((tm, 1), lambda i: (i, 0))
    vec = pl.BlockSpec((1, d), lambda i: (0, 0))
    return pl.pallas_call(
        _combine_ln_kernel,
        grid=(n // tm,),
        in_specs=[rows, rows, rows, col, col, vec, vec],
        out_specs=rows,
        out_shape=jax.ShapeDtypeStruct((n, d), F32),
        compiler_params=_params("parallel"),
        name="combine_ln",
    )(x, y1, y2, g1, g2, g, b)


def _moe(x, w_router, w_in, w_out, g, b):
    n, d = x.shape
    tm = MOE_TM
    info = _router(x, w_router.T, 1024)
    idx = info[:2].astype(jnp.int32).reshape(2 * n)
    gates = info[2:4]
    onehot = (idx[:, None] == jnp.arange(N_EXPERTS, dtype=jnp.int32)[None, :]).astype(jnp.int32)
    csum = jnp.cumsum(onehot, axis=0)
    rank = jnp.sum(csum * onehot, axis=1) - 1
    counts = csum[-1]
    padded = ((counts + tm - 1) // tm) * tm
    ends = jnp.cumsum(padded)
    starts = ends - padded
    pos = starts[idx] + rank
    p = 2 * n + N_EXPERTS * tm
    token = jnp.arange(2 * n, dtype=jnp.int32) % n
    token_of_pos = jnp.zeros((p,), jnp.int32).at[pos].set(token)
    tile_start = jnp.arange(p // tm, dtype=jnp.int32) * tm
    tile_expert = jnp.minimum(jnp.searchsorted(ends, tile_start, side="right"),
                              N_EXPERTS - 1).astype(jnp.int32)
    n_used = (ends[-1] // tm).astype(jnp.int32).reshape(1)
    xs = jnp.take(x.astype(BF16), token_of_pos, axis=0)
    ys = _experts(xs, tile_expert, n_used, w_in, w_out)
    y1 = jnp.take(ys, pos[:n], axis=0)
    y2 = jnp.take(ys, pos[n:], axis=0)
    return _combine_ln(x, y1, y2, gates[0][:, None], gates[1][:, None], g, b, 1024)


def kernel(x, ret_w_in, ret_w_out, sb_w_kv, sb_w_q, sb_w_out, ln_mix_g, ln_mix_b,
           ln_ffn_g, ln_ffn_b, ffn_w_in, ffn_w_out, moe_w_router, moe_w_in, moe_w_out):
    batch, seq, d = x.shape
    n = batch * seq
    x0 = x.reshape(n, d)
    row = lambda v: v.reshape(1, d)

    proj = _mm(x0, ret_w_in[0].astype(BF16), 1024, 512, BF16)
    og = _retention(proj, batch, seq)
    x1 = _mm_res_ln(og, ret_w_out[0].astype(BF16), x0, row(ln_mix_g[0]), row(ln_mix_b[0]), 512)
    x2 = _ffn(x1, ffn_w_in[0].astype(BF16), ffn_w_out[0].astype(BF16),
              row(ln_ffn_g[0]), row(ln_ffn_b[0]), 512)

    w_qkv = jnp.concatenate([sb_w_q[0], sb_w_kv], axis=1).astype(BF16)
    qkv = _mm(x2, w_qkv, 1024, 512, BF16)
    att = _sb_attention(qkv, batch, seq)
    x3 = _mm_res_ln(att, sb_w_out[0].astype(BF16), x2, row(ln_mix_g[1]), row(ln_mix_b[1]), 512)
    x4 = _moe(x3, moe_w_router[0], moe_w_in[0].astype(BF16), moe_w_out[0].astype(BF16),
              row(ln_ffn_g[1]), row(ln_ffn_b[1]))
    return x4.reshape(batch, seq, d)
```

```python
import functools
import math

import jax
import jax.numpy as jnp
from jax import lax
from jax.experimental import pallas as pl
from jax.experimental.pallas import tpu as pltpu

D_MODEL = 1024
DEPTH = 2
CHUNK = 64
DEEPNORM_ALPHA = (2.0 * DEPTH) ** 0.25
LN_EPS = 1e-5

RET_HEADS = 4
RET_QK_DIM = D_MODEL // RET_HEADS
RET_V_WIDTH = 2 * D_MODEL
RET_V_DIM = RET_V_WIDTH // RET_HEADS
ROPE_BASE = 10000.0
RET_BLOCK = 256

SB_HEADS = 16
SB_HEAD_DIM = D_MODEL // SB_HEADS
SB_LANES = 256
SB_TILE = 256
SB_ROWS = 256
LOG2_E = math.log2(math.e)
SB_DEAD_LOG2 = 160.0

D_FF = 2816
FFN_TF = 1408
N_EXPERTS = 8
D_EXPERT = 3584
MOE_TF = 512
MOE_TM = 512
GATHER_CHUNK = 512

VMEM_LIMIT = 56 * 1024 * 1024

BF16 = jnp.bfloat16
F32 = jnp.float32


def _params(*sem):
    return pltpu.CompilerParams(dimension_semantics=sem, vmem_limit_bytes=VMEM_LIMIT)


def _dot(a, b):
    return jnp.dot(a, b, preferred_element_type=F32)


def _dot_nt(a, b):
    return lax.dot_general(a, b, (((1,), (1,)), ((), ())), preferred_element_type=F32)


def _dot_tn(a, b):
    return lax.dot_general(a, b, (((0,), (0,)), ((), ())), preferred_element_type=F32)


def _layer_norm(y, g, b):
    mu = jnp.mean(y, axis=-1, keepdims=True)
    yc = y - mu
    var = jnp.mean(yc * yc, axis=-1, keepdims=True)
    return yc * lax.rsqrt(var + LN_EPS) * g + b


def _silu(x):
    return x * (1.0 / (1.0 + jnp.exp(-x)))


def _mm_kernel(a_ref, w_ref, cs_ref, o_ref):
    o_ref[...] = (_dot(a_ref[...].astype(BF16), w_ref[...]) * cs_ref[...]).astype(o_ref.dtype)


def _mm(a, w, col_scale, tm, tn, out_dtype):
    n, k = a.shape
    nout = w.shape[1]
    return pl.pallas_call(
        _mm_kernel,
        grid=(n // tm, nout // tn),
        in_specs=[pl.BlockSpec((tm, k), lambda i, j: (i, 0)),
                  pl.BlockSpec((k, tn), lambda i, j: (0, j)),
                  pl.BlockSpec((1, tn), lambda i, j: (0, j))],
        out_specs=pl.BlockSpec((tm, tn), lambda i, j: (i, j)),
        out_shape=jax.ShapeDtypeStruct((n, nout), out_dtype),
        compiler_params=_params("parallel", "arbitrary"),
        name="proj_mm",
    )(a, w, col_scale)


def _mm_res_ln_kernel(a_ref, w_ref, res_ref, g_ref, b_ref, o_ref):
    y = _dot(a_ref[...].astype(BF16), w_ref[...])
    y = DEEPNORM_ALPHA * res_ref[...] + y
    o_ref[...] = _layer_norm(y, g_ref[...], b_ref[...])


def _mm_res_ln(a, w, res, g, b, tm):
    n, k = a.shape
    d = w.shape[1]
    return pl.pallas_call(
        _mm_res_ln_kernel,
        grid=(n // tm,),
        in_specs=[pl.BlockSpec((tm, k), lambda i: (i, 0)),
                  pl.BlockSpec((k, d), lambda i: (0, 0)),
                  pl.BlockSpec((tm, d), lambda i: (i, 0)),
                  pl.BlockSpec((1, d), lambda i: (0, 0)),
                  pl.BlockSpec((1, d), lambda i: (0, 0))],
        out_specs=pl.BlockSpec((tm, d), lambda i: (i, 0)),
        out_shape=jax.ShapeDtypeStruct((n, d), F32),
        compiler_params=_params("parallel"),
        name="mm_res_ln",
    )(a, w, res, g, b)


def _rotary(x, cos, sin):
    half = x.shape[-1] // 2
    x1, x2 = x[:, :half], x[:, half:]
    return jnp.concatenate([x1 * cos - x2 * sin, x1 * sin + x2 * cos], axis=-1)


def _retention_kernel(q_ref, k_ref, v_ref, g_ref, cos_ref, sin_ref, dmat_ref,
                      qd_ref, kd_ref, cd_ref, o_ref, state_ref):
    @pl.when(pl.program_id(2) == 0)
    def _():
        state_ref[...] = jnp.zeros_like(state_ref)

    cos, sin = cos_ref[...], sin_ref[...]
    q = _rotary(q_ref[...].astype(F32), cos, sin)
    k = _rotary(k_ref[...].astype(F32), cos, sin)
    v = v_ref[...]
    qb = q.astype(BF16)
    state = state_ref[...]
    scores = _dot_nt(qb, k.astype(BF16)) * dmat_ref[0]
    o = _dot(scores.astype(BF16), v) + qd_ref[0] * _dot(qb, state.astype(BF16))
    state_ref[...] = state * cd_ref[0] + _dot_tn((k * kd_ref[0]).astype(BF16), v)
    mu = jnp.mean(o, axis=-1, keepdims=True)
    oc = o - mu
    var = jnp.mean(oc * oc, axis=-1, keepdims=True)
    on = oc * lax.rsqrt(var + LN_EPS)
    o_ref[...] = (_silu(g_ref[...].astype(F32)) * on).astype(o_ref.dtype)


def _retention_tables(seq):
    t = RET_BLOCK
    half = RET_QK_DIM // 2
    inv_freq = 1.0 / (ROPE_BASE ** (jnp.arange(0, RET_QK_DIM, 2, dtype=F32) / RET_QK_DIM))
    ang = jnp.arange(seq, dtype=F32)[:, None] * inv_freq[None, :]
    cos, sin = jnp.cos(ang), jnp.sin(ang)
    assert cos.shape == (seq, half)
    log_g = jnp.log(1.0 - 2.0 ** (-5.0 - jnp.arange(RET_HEADS, dtype=F32)))
    idx = jnp.arange(t, dtype=F32)
    diff = idx[:, None] - idx[None, :]
    chunk = jnp.arange(t) // CHUNK
    same = chunk[:, None] == chunk[None, :]
    earlier = chunk[None, :] < chunk[:, None]
    lg = log_g[:, None, None]
    dmat = jnp.where(same[None], jnp.exp(lg * jnp.abs(diff)[None]),
                     jnp.where(earlier[None], jnp.exp(lg * diff[None]), 0.0))
    qd = jnp.exp(log_g[:, None] * (idx + 1.0))[..., None]
    kd = jnp.exp(log_g[:, None] * (t - 1.0 - idx))[..., None]
    cd = jnp.broadcast_to(jnp.exp(log_g * t)[:, None, None], (RET_HEADS, 1, RET_V_DIM))
    return cos, sin, dmat, qd, kd, cd


def _retention(proj, batch, seq):
    n = proj.shape[0]
    t = RET_BLOCK
    nt = seq // t
    cos, sin, dmat, qd, kd, cd = _retention_tables(seq)
    qk_blocks = D_MODEL // RET_QK_DIM
    v_off = 2 * D_MODEL // RET_V_DIM
    g_off = v_off + RET_V_WIDTH // RET_V_DIM
    half = RET_QK_DIM // 2
    row = lambda b, h, i: b * nt + i
    return pl.pallas_call(
        _retention_kernel,
        grid=(batch, RET_HEADS, nt),
        in_specs=[
            pl.BlockSpec((t, RET_QK_DIM), lambda b, h, i: (row(b, h, i), h)),
            pl.BlockSpec((t, RET_QK_DIM), lambda b, h, i: (row(b, h, i), qk_blocks + h)),
            pl.BlockSpec((t, RET_V_DIM), lambda b, h, i: (row(b, h, i), v_off + h)),
            pl.BlockSpec((t, RET_V_DIM), lambda b, h, i: (row(b, h, i), g_off + h)),
            pl.BlockSpec((t, half), lambda b, h, i: (i, 0)),
            pl.BlockSpec((t, half), lambda b, h, i: (i, 0)),
            pl.BlockSpec((1, t, t), lambda b, h, i: (h, 0, 0)),
            pl.BlockSpec((1, t, 1), lambda b, h, i: (h, 0, 0)),
            pl.BlockSpec((1, t, 1), lambda b, h, i: (h, 0, 0)),
            pl.BlockSpec((1, 1, RET_V_DIM), lambda b, h, i: (h, 0, 0)),
        ],
        out_specs=pl.BlockSpec((t, RET_V_DIM), lambda b, h, i: (row(b, h, i), h)),
        out_shape=jax.ShapeDtypeStruct((n, RET_V_WIDTH), BF16),
        scratch_shapes=[pltpu.VMEM((RET_QK_DIM, RET_V_DIM), F32)],
        compiler_params=_params("parallel", "parallel", "arbitrary"),
        name="retention",
    )(proj, proj, proj, proj, cos, sin, dmat, qd, kd, cd)


def _ffn_kernel(x_ref, wg_ref, wu_ref, wo_ref, g_ref, b_ref, o_ref, acc_ref):
    j = pl.program_id(1)
    xb = x_ref[...].astype(BF16)
    act = _silu(_dot(xb, wg_ref[...])) * _dot(xb, wu_ref[...])
    part = _dot(act.astype(BF16), wo_ref[...])

    @pl.when(j == 0)
    def _():
        acc_ref[...] = part

    @pl.when(j > 0)
    def _():
        acc_ref[...] += part

    @pl.when(j == pl.num_programs(1) - 1)
    def _():
        y = DEEPNORM_ALPHA * x_ref[...] + acc_ref[...]
        o_ref[...] = _layer_norm(y, g_ref[...], b_ref[...])


def _ffn(x, w_in, w_out, g, b, tm):
    n, d = x.shape
    nf = D_FF // FFN_TF
    return pl.pallas_call(
        _ffn_kernel,
        grid=(n // tm, nf),
        in_specs=[pl.BlockSpec((tm, d), lambda i, j: (i, 0)),
                  pl.BlockSpec((d, FFN_TF), lambda i, j: (0, j)),
                  pl.BlockSpec((d, FFN_TF), lambda i, j: (0, nf + j)),
                  pl.BlockSpec((FFN_TF, d), lambda i, j: (j, 0)),
                  pl.BlockSpec((1, d), lambda i, j: (0, 0)),
                  pl.BlockSpec((1, d), lambda i, j: (0, 0))],
        out_specs=pl.BlockSpec((tm, d), lambda i, j: (i, 0)),
        out_shape=jax.ShapeDtypeStruct((n, d), F32),
        scratch_shapes=[pltpu.VMEM((tm, d), F32)],
        compiler_params=_params("parallel", "arbitrary"),
        name="ffn",
    )(x, w_in, w_in, w_out, g, b)


def _softplus2(z):
    neg_abs = lax.bitcast_convert_type(
        lax.bitcast_convert_type(z, jnp.uint32) | jnp.uint32(0x80000000), F32)
    return jnp.maximum(z, 0.0) + jnp.log(1.0 + jnp.exp2(neg_abs)) * LOG2_E


def _sb_kernel(q_ref, k_ref, v_ref, o_ref, vm_ref):
    i = pl.program_id(2)
    t = SB_TILE
    r = SB_ROWS
    nh = SB_LANES // SB_HEAD_DIM
    lane = lax.broadcasted_iota(jnp.int32, (1, SB_LANES), 1)
    in_head = [(lane >= h * SB_HEAD_DIM) & (lane < (h + 1) * SB_HEAD_DIM) for h in range(nh)]

    @pl.when(i == 0)
    def _():
        v_all = v_ref[...]
        for h in range(nh):
            vm_ref[h] = jnp.where(in_head[h], v_all, jnp.zeros_like(v_all))

    q = q_ref[...]
    qm = [jnp.where(in_head[h], q, jnp.zeros_like(q)) for h in range(nh)]
    row = lax.broadcasted_iota(jnp.int32, (t, t), 0)
    col = lax.broadcasted_iota(jnp.int32, (t, t), 1)
    neg_suffix = jnp.where(row > col, -1.0, 0.0).astype(BF16)
    causal = col < row

    def tile(j, runs, accs, diag):
        start = pl.multiple_of(j * t, t)
        kt = k_ref[pl.ds(start, t), :]
        new_runs = [[None] * (t // r) for _ in range(nh)]
        new_accs = list(accs)
        for s in range(t // r):
            rows = slice(s * r, (s + 1) * r)
            zs = [_dot_nt(qm[h][rows], kt) for h in range(nh)]
            sps = [_softplus2(z) for z in zs]
            spms = [jnp.where(causal[rows], sp, 0.0) if diag else sp for sp in sps]
            betweens = [_dot(spms[h].astype(BF16), neg_suffix) - runs[h][s] for h in range(nh)]
            ws = [jnp.exp2(zs[h] - sps[h] + betweens[h]) for h in range(nh)]
            if diag:
                ws = [jnp.where(causal[rows], w, 0.0) for w in ws]
            pv = [_dot(ws[h].astype(BF16), vm_ref[h, pl.ds(start, t), :]) for h in range(nh)]
            new_accs[s] = new_accs[s] + ((pv[0] + pv[1]) + (pv[2] + pv[3]))
            for h in range(nh):
                new_runs[h][s] = runs[h][s] + jnp.sum(spms[h], axis=1, keepdims=True)
        return new_runs, new_accs

    runs0 = [[jnp.zeros((r, 1), F32) for _ in range(t // r)] for _ in range(nh)]
    accs0 = [jnp.zeros((r, SB_LANES), F32) for _ in range(t // r)]
    runs, accs = tile(i, runs0, accs0, True)

    def all_dead(runs):
        m = functools.reduce(jnp.minimum, [x for per_head in runs for x in per_head])
        return (jnp.min(m) > SB_DEAD_LOG2).astype(jnp.int32)

    def cond(c):
        return jnp.logical_and(c[0] >= 0, c[1] == 0)

    def body(c):
        j, _, runs, accs = c
        runs, accs = tile(j, runs, accs, False)
        return j - 1, all_dead(runs), runs, accs

    _, _, runs, accs = lax.while_loop(cond, body, (i - 1, all_dead(runs), runs, accs))
    for s in range(t // r):
        o_ref[s * r:(s + 1) * r, :] = accs[s].astype(o_ref.dtype)


def _sb_attention(qkv, batch, seq):
    n = qkv.shape[0]
    t = SB_TILE
    nt = seq // t
    ng = D_MODEL // SB_LANES
    nh = SB_LANES // SB_HEAD_DIM
    return pl.pallas_call(
        _sb_kernel,
        grid=(batch, ng, nt),
        in_specs=[
            pl.BlockSpec((t, SB_LANES), lambda b, g, i: (b * nt + i, g)),
            pl.BlockSpec((seq, SB_LANES), lambda b, g, i: (b, ng + g)),
            pl.BlockSpec((seq, SB_LANES), lambda b, g, i: (b, 2 * ng + g)),
        ],
        out_specs=pl.BlockSpec((t, SB_LANES), lambda b, g, i: (b * nt + i, g)),
        out_shape=jax.ShapeDtypeStruct((n, D_MODEL), BF16),
        scratch_shapes=[pltpu.VMEM((nh, seq, SB_LANES), BF16)],
        compiler_params=_params("arbitrary", "arbitrary", "arbitrary"),
        name="sb_attention",
    )(qkv, qkv, qkv)


def _router_kernel(x_ref, wr_ref, o_ref):
    logits = lax.dot_general(wr_ref[...], x_ref[...], (((1,), (1,)), ((), ())),
                             precision=lax.Precision.HIGHEST,
                             preferred_element_type=F32)
    e_idx = lax.broadcasted_iota(jnp.int32, logits.shape, 0)
    m1 = jnp.max(logits, axis=0, keepdims=True)
    i1 = jnp.min(jnp.where(logits == m1, e_idx, N_EXPERTS), axis=0, keepdims=True)
    rest = jnp.where(e_idx == i1, -jnp.inf, logits)
    m2 = jnp.max(rest, axis=0, keepdims=True)
    i2 = jnp.min(jnp.where(rest == m2, e_idx, N_EXPERTS), axis=0, keepdims=True)
    e2 = jnp.exp(m2 - m1)
    den = 1.0 + e2
    g1 = 1.0 / den
    g2 = e2 / den
    zeros = jnp.zeros((N_EXPERTS - 4, logits.shape[1]), F32)
    o_ref[...] = jnp.concatenate([i1.astype(F32), i2.astype(F32), g1, g2, zeros], axis=0)


def _router(x, w_router_t, tm):
    n, d = x.shape
    return pl.pallas_call(
        _router_kernel,
        grid=(n // tm,),
        in_specs=[pl.BlockSpec((tm, d), lambda i: (i, 0)),
                  pl.BlockSpec((N_EXPERTS, d), lambda i: (0, 0))],
        out_specs=pl.BlockSpec((N_EXPERTS, tm), lambda i: (0, i)),
        out_shape=jax.ShapeDtypeStruct((N_EXPERTS, n), F32),
        compiler_params=_params("parallel"),
        name="router",
    )(x, w_router_t)


def _gather_copy(idx_ref, src_ref, out_ref, sems, chunk, r):
    row = chunk * GATHER_CHUNK + r
    return pltpu.make_async_copy(src_ref.at[pl.ds(idx_ref[row], 1)],
                                 out_ref.at[pl.ds(row, 1)], sems.at[chunk % 2])


def _gather_kernel(idx_ref, src_ref, out_ref, sems):
    c = pl.program_id(0)

    def start(r, carry):
        _gather_copy(idx_ref, src_ref, out_ref, sems, c, r).start()
        return carry

    lax.fori_loop(0, GATHER_CHUNK, start, 0, unroll=8)

    def wait_chunk(chunk):
        pltpu.make_async_copy(src_ref.at[pl.ds(0, GATHER_CHUNK)],
                              out_ref.at[pl.ds(chunk * GATHER_CHUNK, GATHER_CHUNK)],
                              sems.at[chunk % 2]).wait()

    @pl.when(c > 0)
    def _():
        wait_chunk(c - 1)

    @pl.when(c == pl.num_programs(0) - 1)
    def _():
        wait_chunk(c)


def _gather_rows(src, idx):
    rows = idx.shape[0]
    d = src.shape[1]
    grid_spec = pltpu.PrefetchScalarGridSpec(
        num_scalar_prefetch=1,
        grid=(rows // GATHER_CHUNK,),
        in_specs=[pl.BlockSpec(memory_space=pl.ANY)],
        out_specs=pl.BlockSpec(memory_space=pl.ANY),
        scratch_shapes=[pltpu.SemaphoreType.DMA((2,))],
    )
    return pl.pallas_call(
        _gather_kernel,
        grid_spec=grid_spec,
        out_shape=jax.ShapeDtypeStruct((rows, d), src.dtype),
        compiler_params=_params("arbitrary"),
        name="gather_rows",
    )(idx, src)


def _expert_kernel(te_ref, nu_ref, x_ref, wg_ref, wu_ref, wo_ref, o_ref, acc_ref):
    i = pl.program_id(0)
    j = pl.program_id(1)

    @pl.when(i < nu_ref[0])
    def _():
        xb = x_ref[...].astype(BF16)
        act = _silu(_dot(xb, wg_ref[0])) * _dot(xb, wu_ref[0])
        part = _dot(act.astype(BF16), wo_ref[0])

        @pl.when(j == 0)
        def _():
            acc_ref[...] = part

        @pl.when(j > 0)
        def _():
            acc_ref[...] += part

    @pl.when(j == pl.num_programs(1) - 1)
    def _():
        o_ref[...] = acc_ref[...]


def _experts(xs, tile_expert, n_used, w_in, w_out):
    p, d = xs.shape
    tm = MOE_TM
    nf = D_EXPERT // MOE_TF
    grid_spec = pltpu.PrefetchScalarGridSpec(
        num_scalar_prefetch=2,
        grid=(p // tm, nf),
        in_specs=[
            pl.BlockSpec((tm, d), lambda i, j, te, nu: (i, 0)),
            pl.BlockSpec((1, d, MOE_TF), lambda i, j, te, nu: (te[i], 0, j)),
            pl.BlockSpec((1, d, MOE_TF), lambda i, j, te, nu: (te[i], 0, nf + j)),
            pl.BlockSpec((1, MOE_TF, d), lambda i, j, te, nu: (te[i], j, 0)),
        ],
        out_specs=pl.BlockSpec((tm, d), lambda i, j, te, nu: (i, 0)),
        scratch_shapes=[pltpu.VMEM((tm, d), F32)],
    )
    return pl.pallas_call(
        _expert_kernel,
        grid_spec=grid_spec,
        out_shape=jax.ShapeDtypeStruct((p, d), F32),
        compiler_params=_params("arbitrary", "arbitrary"),
        name="experts",
    )(tile_expert, n_used, xs, w_in, w_in, w_out)


def _combine_ln_kernel(x_ref, y1_ref, y2_ref, g1_ref, g2_ref, g_ref, b_ref, o_ref):
    y = (DEEPNORM_ALPHA * x_ref[...] + g1_ref[...] * y1_ref[...]) + g2_ref[...] * y2_ref[...]
    o_ref[...] = _layer_norm(y, g_ref[...], b_ref[...])


def _combine_ln(x, y12, g1, g2, g, b, tm):
    n, d = x.shape
    rows = pl.BlockSpec((tm, d), lambda i: (i, 0))
    rows2 = pl.BlockSpec((tm, d), lambda i: (n // tm + i, 0))
    col = pl.BlockSpec((tm, 1), lambda i: (i, 0))
    vec = pl.BlockSpec((1, d), lambda i: (0, 0))
    return pl.pallas_call(
        _combine_ln_kernel,
        grid=(n // tm,),
        in_specs=[rows, rows, rows2, col, col, vec, vec],
        out_specs=rows,
        out_shape=jax.ShapeDtypeStruct((n, d), F32),
        compiler_params=_params("parallel"),
        name="combine_ln",
    )(x, y12, y12, g1, g2, g, b)


def _moe(x, w_router, w_in, w_out, g, b):
    n, d = x.shape
    tm = MOE_TM
    info = _router(x, w_router.T, 1024)
    idx = info[:2].astype(jnp.int32).reshape(2 * n)
    gates = info[2:4]
    onehot = (idx[:, None] == jnp.arange(N_EXPERTS, dtype=jnp.int32)[None, :]).astype(jnp.int32)
    csum = jnp.cumsum(onehot, axis=0)
    rank = jnp.sum(csum * onehot, axis=1) - 1
    counts = csum[-1]
    padded = ((counts + tm - 1) // tm) * tm
    ends = jnp.cumsum(padded)
    starts = ends - padded
    pos = starts[idx] + rank
    p = 2 * n + N_EXPERTS * tm
    token = jnp.arange(2 * n, dtype=jnp.int32) % n
    token_of_pos = jnp.zeros((p,), jnp.int32).at[pos].set(token)
    tile_start = jnp.arange(p // tm, dtype=jnp.int32) * tm
    tile_expert = jnp.minimum(
        jnp.sum((tile_start[:, None] >= ends[None, :]).astype(jnp.int32), axis=1), N_EXPERTS - 1)
    n_used = (ends[-1] // tm).astype(jnp.int32).reshape(1)
    xs = _gather_rows(x, token_of_pos)
    ys = _experts(xs, tile_expert, n_used, w_in, w_out)
    y12 = _gather_rows(ys, pos)
    return _combine_ln(x, y12, gates[0][:, None], gates[1][:, None], g, b, 1024)


def kernel(x, ret_w_in, ret_w_out, sb_w_kv, sb_w_q, sb_w_out, ln_mix_g, ln_mix_b,
           ln_ffn_g, ln_ffn_b, ffn_w_in, ffn_w_out, moe_w_router, moe_w_in, moe_w_out):
    batch, seq, d = x.shape
    n = batch * seq
    x0 = x.reshape(n, d)
    row = lambda v: v.reshape(1, d)

    ret_scale = jnp.concatenate([
        jnp.ones((1, D_MODEL), F32), jnp.full((1, D_MODEL), RET_QK_DIM ** -0.5, F32),
        jnp.ones((1, 2 * RET_V_WIDTH), F32)], axis=1)
    proj = _mm(x0, ret_w_in[0].astype(BF16), ret_scale, 1024, 512, BF16)
    og = _retention(proj, batch, seq)
    x1 = _mm_res_ln(og, ret_w_out[0].astype(BF16), x0, row(ln_mix_g[0]), row(ln_mix_b[0]), 512)
    x2 = _ffn(x1, ffn_w_in[0].astype(BF16), ffn_w_out[0].astype(BF16),
              row(ln_ffn_g[0]), row(ln_ffn_b[0]), 512)

    w_qkv = jnp.concatenate([sb_w_q[0], sb_w_kv], axis=1).astype(BF16)
    sb_scale = jnp.concatenate([
        jnp.full((1, D_MODEL), SB_HEAD_DIM ** -0.5 * LOG2_E, F32),
        jnp.ones((1, 2 * D_MODEL), F32)], axis=1)
    qkv = _mm(x2, w_qkv, sb_scale, 1024, 512, BF16)
    att = _sb_attention(qkv, batch, seq)
    x3 = _mm_res_ln(att, sb_w_out[0].astype(BF16), x2, row(ln_mix_g[1]), row(ln_mix_b[1]), 512)
    x4 = _moe(x3, moe_w_router[0], moe_w_in[0].astype(BF16), moe_w_out[0].astype(BF16),
              row(ln_ffn_g[1]), row(ln_ffn_b[1]))
    return x4.reshape(batch, seq, d)
```

```python
import functools
import math

import jax
import jax.numpy as jnp
from jax import lax
from jax.experimental import pallas as pl
from jax.experimental.pallas import tpu as pltpu

D_MODEL = 1024
DEPTH = 2
LANES = 128
CHUNK = 64
DEEPNORM_ALPHA = (2.0 * DEPTH) ** 0.25
LN_EPS = 1e-5

RET_HEADS = 4
RET_QK_DIM = D_MODEL // RET_HEADS
RET_V_WIDTH = 2 * D_MODEL
RET_V_DIM = RET_V_WIDTH // RET_HEADS
ROPE_BASE = 10000.0
RET_BLOCK = 256

SB_HEADS = 16
SB_HEAD_DIM = D_MODEL // SB_HEADS
SB_LANES = 256
SB_TILE = 256
SB_ROWS = 256
LOG2_E = math.log2(math.e)
SB_DEAD_LOG2 = 160.0

D_FF = 2816
FFN_TF = 1408
N_EXPERTS = 8
D_EXPERT = 3584
MOE_TF = 512
MOE_TM = 512
GATHER_CHUNK = 512

VMEM_LIMIT = 56 * 1024 * 1024

BF16 = jnp.bfloat16
F32 = jnp.float32


def _params(*sem):
    return pltpu.CompilerParams(dimension_semantics=sem, vmem_limit_bytes=VMEM_LIMIT)


def _dot(a, b):
    return jnp.dot(a, b, preferred_element_type=F32)


def _dot_nt(a, b):
    return lax.dot_general(a, b, (((1,), (1,)), ((), ())), preferred_element_type=F32)


def _dot_tn(a, b):
    return lax.dot_general(a, b, (((0,), (0,)), ((), ())), preferred_element_type=F32)


def _layer_norm(y, g, b):
    mu = jnp.mean(y, axis=-1, keepdims=True)
    yc = y - mu
    var = jnp.mean(yc * yc, axis=-1, keepdims=True)
    return yc * lax.rsqrt(var + LN_EPS) * g + b


def _silu(x):
    return x * (1.0 / (1.0 + jnp.exp(-x)))


def _store_row_tiles(o_ref, y):
    for s in range(o_ref.shape[1]):
        o_ref[:, s, :] = y[:, s * LANES:(s + 1) * LANES].astype(o_ref.dtype)


def _load_row_tiles(x_ref):
    return jnp.concatenate([x_ref[:, s, :] for s in range(x_ref.shape[1])], axis=-1)


def _mm_kernel(a_ref, w_ref, cs_ref, o_ref):
    o_ref[...] = (_dot(a_ref[...].astype(BF16), w_ref[...]) * cs_ref[...]).astype(o_ref.dtype)


def _mm(a, w, col_scale, tm, tn, out_dtype):
    n, k = a.shape
    nout = w.shape[1]
    return pl.pallas_call(
        _mm_kernel,
        grid=(n // tm, nout // tn),
        in_specs=[pl.BlockSpec((tm, k), lambda i, j: (i, 0)),
                  pl.BlockSpec((k, tn), lambda i, j: (0, j)),
                  pl.BlockSpec((1, tn), lambda i, j: (0, j))],
        out_specs=pl.BlockSpec((tm, tn), lambda i, j: (i, j)),
        out_shape=jax.ShapeDtypeStruct((n, nout), out_dtype),
        compiler_params=_params("parallel", "arbitrary"),
        name="proj_mm",
    )(a, w, col_scale)


def _mm_res_ln_kernel(a_ref, w_ref, res_ref, g_ref, b_ref, o_ref):
    y = _dot(a_ref[...].astype(BF16), w_ref[...])
    y = DEEPNORM_ALPHA * res_ref[...] + y
    out = _layer_norm(y, g_ref[...], b_ref[...])
    if len(o_ref.shape) == 3:
        _store_row_tiles(o_ref, out)
    else:
        o_ref[...] = out


def _mm_res_ln(a, w, res, g, b, tm, row_tiles=False):
    n, k = a.shape
    d = w.shape[1]
    if row_tiles:
        out_spec = pl.BlockSpec((tm, d // LANES, LANES), lambda i: (i, 0, 0))
        out_shape = jax.ShapeDtypeStruct((n, d // LANES, LANES), F32)
    else:
        out_spec = pl.BlockSpec((tm, d), lambda i: (i, 0))
        out_shape = jax.ShapeDtypeStruct((n, d), F32)
    return pl.pallas_call(
        _mm_res_ln_kernel,
        grid=(n // tm,),
        in_specs=[pl.BlockSpec((tm, k), lambda i: (i, 0)),
                  pl.BlockSpec((k, d), lambda i: (0, 0)),
                  pl.BlockSpec((tm, d), lambda i: (i, 0)),
                  pl.BlockSpec((1, d), lambda i: (0, 0)),
                  pl.BlockSpec((1, d), lambda i: (0, 0))],
        out_specs=out_spec,
        out_shape=out_shape,
        compiler_params=_params("parallel"),
        name="mm_res_ln",
    )(a, w, res, g, b)


def _rotary(x, cos, sin):
    half = x.shape[-1] // 2
    x1, x2 = x[:, :half], x[:, half:]
    return jnp.concatenate([x1 * cos - x2 * sin, x1 * sin + x2 * cos], axis=-1)


def _retention_kernel(q_ref, k_ref, v_ref, g_ref, cos_ref, sin_ref, dmat_ref,
                      qd_ref, kd_ref, cd_ref, o_ref, state_ref):
    @pl.when(pl.program_id(2) == 0)
    def _():
        state_ref[...] = jnp.zeros_like(state_ref)

    cos, sin = cos_ref[...], sin_ref[...]
    q = _rotary(q_ref[...].astype(F32), cos, sin)
    k = _rotary(k_ref[...].astype(F32), cos, sin)
    v = v_ref[...]
    qb = q.astype(BF16)
    state = state_ref[...]
    scores = _dot_nt(qb, k.astype(BF16)) * dmat_ref[0]
    o = _dot(scores.astype(BF16), v) + qd_ref[0] * _dot(qb, state.astype(BF16))
    state_ref[...] = state * cd_ref[0] + _dot_tn((k * kd_ref[0]).astype(BF16), v)
    mu = jnp.mean(o, axis=-1, keepdims=True)
    oc = o - mu
    var = jnp.mean(oc * oc, axis=-1, keepdims=True)
    on = oc * lax.rsqrt(var + LN_EPS)
    o_ref[...] = (_silu(g_ref[...].astype(F32)) * on).astype(o_ref.dtype)


def _retention_tables(seq):
    t = RET_BLOCK
    half = RET_QK_DIM // 2
    inv_freq = 1.0 / (ROPE_BASE ** (jnp.arange(0, RET_QK_DIM, 2, dtype=F32) / RET_QK_DIM))
    ang = jnp.arange(seq, dtype=F32)[:, None] * inv_freq[None, :]
    cos, sin = jnp.cos(ang), jnp.sin(ang)
    assert cos.shape == (seq, half)
    log_g = jnp.log(1.0 - 2.0 ** (-5.0 - jnp.arange(RET_HEADS, dtype=F32)))
    idx = jnp.arange(t, dtype=F32)
    diff = idx[:, None] - idx[None, :]
    chunk = jnp.arange(t) // CHUNK
    same = chunk[:, None] == chunk[None, :]
    earlier = chunk[None, :] < chunk[:, None]
    lg = log_g[:, None, None]
    dmat = jnp.where(same[None], jnp.exp(lg * jnp.abs(diff)[None]),
                     jnp.where(earlier[None], jnp.exp(lg * diff[None]), 0.0))
    qd = jnp.exp(log_g[:, None] * (idx + 1.0))[..., None]
    kd = jnp.exp(log_g[:, None] * (t - 1.0 - idx))[..., None]
    cd = jnp.broadcast_to(jnp.exp(log_g * t)[:, None, None], (RET_HEADS, 1, RET_V_DIM))
    return cos, sin, dmat, qd, kd, cd


def _retention(proj, batch, seq):
    n = proj.shape[0]
    t = RET_BLOCK
    nt = seq // t
    cos, sin, dmat, qd, kd, cd = _retention_tables(seq)
    qk_blocks = D_MODEL // RET_QK_DIM
    v_off = 2 * D_MODEL // RET_V_DIM
    g_off = v_off + RET_V_WIDTH // RET_V_DIM
    half = RET_QK_DIM // 2
    row = lambda b, h, i: b * nt + i
    return pl.pallas_call(
        _retention_kernel,
        grid=(batch, RET_HEADS, nt),
        in_specs=[
            pl.BlockSpec((t, RET_QK_DIM), lambda b, h, i: (row(b, h, i), h)),
            pl.BlockSpec((t, RET_QK_DIM), lambda b, h, i: (row(b, h, i), qk_blocks + h)),
            pl.BlockSpec((t, RET_V_DIM), lambda b, h, i: (row(b, h, i), v_off + h)),
            pl.BlockSpec((t, RET_V_DIM), lambda b, h, i: (row(b, h, i), g_off + h)),
            pl.BlockSpec((t, half), lambda b, h, i: (i, 0)),
            pl.BlockSpec((t, half), lambda b, h, i: (i, 0)),
            pl.BlockSpec((1, t, t), lambda b, h, i: (h, 0, 0)),
            pl.BlockSpec((1, t, 1), lambda b, h, i: (h, 0, 0)),
            pl.BlockSpec((1, t, 1), lambda b, h, i: (h, 0, 0)),
            pl.BlockSpec((1, 1, RET_V_DIM), lambda b, h, i: (h, 0, 0)),
        ],
        out_specs=pl.BlockSpec((t, RET_V_DIM), lambda b, h, i: (row(b, h, i), h)),
        out_shape=jax.ShapeDtypeStruct((n, RET_V_WIDTH), BF16),
        scratch_shapes=[pltpu.VMEM((RET_QK_DIM, RET_V_DIM), F32)],
        compiler_params=_params("parallel", "parallel", "arbitrary"),
        name="retention",
    )(proj, proj, proj, proj, cos, sin, dmat, qd, kd, cd)


def _ffn_kernel(x_ref, wg_ref, wu_ref, wo_ref, g_ref, b_ref, o_ref, acc_ref):
    j = pl.program_id(1)
    xb = x_ref[...].astype(BF16)
    act = _silu(_dot(xb, wg_ref[...])) * _dot(xb, wu_ref[...])
    part = _dot(act.astype(BF16), wo_ref[...])

    @pl.when(j == 0)
    def _():
        acc_ref[...] = part

    @pl.when(j > 0)
    def _():
        acc_ref[...] += part

    @pl.when(j == pl.num_programs(1) - 1)
    def _():
        y = DEEPNORM_ALPHA * x_ref[...] + acc_ref[...]
        o_ref[...] = _layer_norm(y, g_ref[...], b_ref[...])


def _ffn(x, w_in, w_out, g, b, tm):
    n, d = x.shape
    nf = D_FF // FFN_TF
    return pl.pallas_call(
        _ffn_kernel,
        grid=(n // tm, nf),
        in_specs=[pl.BlockSpec((tm, d), lambda i, j: (i, 0)),
                  pl.BlockSpec((d, FFN_TF), lambda i, j: (0, j)),
                  pl.BlockSpec((d, FFN_TF), lambda i, j: (0, nf + j)),
                  pl.BlockSpec((FFN_TF, d), lambda i, j: (j, 0)),
                  pl.BlockSpec((1, d), lambda i, j: (0, 0)),
                  pl.BlockSpec((1, d), lambda i, j: (0, 0))],
        out_specs=pl.BlockSpec((tm, d), lambda i, j: (i, 0)),
        out_shape=jax.ShapeDtypeStruct((n, d), F32),
        scratch_shapes=[pltpu.VMEM((tm, d), F32)],
        compiler_params=_params("parallel", "arbitrary"),
        name="ffn",
    )(x, w_in, w_in, w_out, g, b)


def _softplus2(z):
    neg_abs = lax.bitcast_convert_type(
        lax.bitcast_convert_type(z, jnp.uint32) | jnp.uint32(0x80000000), F32)
    return jnp.maximum(z, 0.0) + jnp.log(1.0 + jnp.exp2(neg_abs)) * LOG2_E


def _sb_kernel(q_ref, k_ref, v_ref, o_ref, vm_ref):
    i = pl.program_id(2)
    t = SB_TILE
    r = SB_ROWS
    nh = SB_LANES // SB_HEAD_DIM
    lane = lax.broadcasted_iota(jnp.int32, (1, SB_LANES), 1)
    in_head = [(lane >= h * SB_HEAD_DIM) & (lane < (h + 1) * SB_HEAD_DIM) for h in range(nh)]

    @pl.when(i == 0)
    def _():
        v_all = v_ref[...]
        for h in range(nh):
            vm_ref[h] = jnp.where(in_head[h], v_all, jnp.zeros_like(v_all))

    q = q_ref[...]
    qm = [jnp.where(in_head[h], q, jnp.zeros_like(q)) for h in range(nh)]
    row = lax.broadcasted_iota(jnp.int32, (t, t), 0)
    col = lax.broadcasted_iota(jnp.int32, (t, t), 1)
    neg_suffix = jnp.where(row > col, -1.0, 0.0).astype(BF16)
    causal = col < row

    def tile(j, runs, accs, diag):
        start = pl.multiple_of(j * t, t)
        kt = k_ref[pl.ds(start, t), :]
        new_runs = [[None] * (t // r) for _ in range(nh)]
        new_accs = list(accs)
        for s in range(t // r):
            rows = slice(s * r, (s + 1) * r)
            zs = [_dot_nt(qm[h][rows], kt) for h in range(nh)]
            sps = [_softplus2(z) for z in zs]
            spms = [jnp.where(causal[rows], sp, 0.0) if diag else sp for sp in sps]
            betweens = [_dot(spms[h].astype(BF16), neg_suffix) - runs[h][s] for h in range(nh)]
            ws = [jnp.exp2(zs[h] - sps[h] + betweens[h]) for h in range(nh)]
            if diag:
                ws = [jnp.where(causal[rows], w, 0.0) for w in ws]
            pv = [_dot(ws[h].astype(BF16), vm_ref[h, pl.ds(start, t), :]) for h in range(nh)]
            new_accs[s] = new_accs[s] + ((pv[0] + pv[1]) + (pv[2] + pv[3]))
            for h in range(nh):
                new_runs[h][s] = runs[h][s] + jnp.sum(spms[h], axis=1, keepdims=True)
        return new_runs, new_accs

    runs0 = [[jnp.zeros((r, 1), F32) for _ in range(t // r)] for _ in range(nh)]
    accs0 = [jnp.zeros((r, SB_LANES), F32) for _ in range(t // r)]
    runs, accs = tile(i, runs0, accs0, True)

    def all_dead(runs):
        m = functools.reduce(jnp.minimum, [x for per_head in runs for x in per_head])
        return (jnp.min(m) > SB_DEAD_LOG2).astype(jnp.int32)

    def cond(c):
        return jnp.logical_and(c[0] >= 0, c[1] == 0)

    def body(c):
        j, _, runs, accs = c
        runs, accs = tile(j, runs, accs, False)
        return j - 1, all_dead(runs), runs, accs

    _, _, runs, accs = lax.while_loop(cond, body, (i - 1, all_dead(runs), runs, accs))
    for s in range(t // r):
        o_ref[s * r:(s + 1) * r, :] = accs[s].astype(o_ref.dtype)


def _sb_attention(qkv, batch, seq):
    n = qkv.shape[0]
    t = SB_TILE
    nt = seq // t
    ng = D_MODEL // SB_LANES
    nh = SB_LANES // SB_HEAD_DIM
    return pl.pallas_call(
        _sb_kernel,
        grid=(batch, ng, nt),
        in_specs=[
            pl.BlockSpec((t, SB_LANES), lambda b, g, i: (b * nt + i, g)),
            pl.BlockSpec((seq, SB_LANES), lambda b, g, i: (b, ng + g)),
            pl.BlockSpec((seq, SB_LANES), lambda b, g, i: (b, 2 * ng + g)),
        ],
        out_specs=pl.BlockSpec((t, SB_LANES), lambda b, g, i: (b * nt + i, g)),
        out_shape=jax.ShapeDtypeStruct((n, D_MODEL), BF16),
        scratch_shapes=[pltpu.VMEM((nh, seq, SB_LANES), BF16)],
        compiler_params=_params("arbitrary", "arbitrary", "arbitrary"),
        name="sb_attention",
    )(qkv, qkv, qkv)


def _router_kernel(x_ref, wr_ref, o_ref):
    logits = lax.dot_general(wr_ref[...], _load_row_tiles(x_ref), (((1,), (1,)), ((), ())),
                             precision=lax.Precision.HIGHEST,
                             preferred_element_type=F32)
    e_idx = lax.broadcasted_iota(jnp.int32, logits.shape, 0)
    m1 = jnp.max(logits, axis=0, keepdims=True)
    i1 = jnp.min(jnp.where(logits == m1, e_idx, N_EXPERTS), axis=0, keepdims=True)
    rest = jnp.where(e_idx == i1, -jnp.inf, logits)
    m2 = jnp.max(rest, axis=0, keepdims=True)
    i2 = jnp.min(jnp.where(rest == m2, e_idx, N_EXPERTS), axis=0, keepdims=True)
    e2 = jnp.exp(m2 - m1)
    den = 1.0 + e2
    g1 = 1.0 / den
    g2 = e2 / den
    zeros = jnp.zeros((N_EXPERTS - 4, logits.shape[1]), F32)
    o_ref[...] = jnp.concatenate([i1.astype(F32), i2.astype(F32), g1, g2, zeros], axis=0)


def _router(x, w_router_t, tm):
    n, nl, _ = x.shape
    return pl.pallas_call(
        _router_kernel,
        grid=(n // tm,),
        in_specs=[pl.BlockSpec((tm, nl, LANES), lambda i: (i, 0, 0)),
                  pl.BlockSpec((N_EXPERTS, nl * LANES), lambda i: (0, 0))],
        out_specs=pl.BlockSpec((N_EXPERTS, tm), lambda i: (0, i)),
        out_shape=jax.ShapeDtypeStruct((N_EXPERTS, n), F32),
        compiler_params=_params("parallel"),
        name="router",
    )(x, w_router_t)


def _gather_copy(idx_ref, src_ref, out_ref, sems, chunk, r):
    row = chunk * GATHER_CHUNK + r
    return pltpu.make_async_copy(src_ref.at[idx_ref[row]], out_ref.at[row], sems.at[chunk % 2])


def _gather_kernel(idx_ref, src_ref, out_ref, sems):
    c = pl.program_id(0)

    def start(r, carry):
        _gather_copy(idx_ref, src_ref, out_ref, sems, c, r).start()
        return carry

    lax.fori_loop(0, GATHER_CHUNK, start, 0, unroll=8)

    def wait_chunk(chunk):
        pltpu.make_async_copy(src_ref.at[pl.ds(0, GATHER_CHUNK)],
                              out_ref.at[pl.ds(chunk * GATHER_CHUNK, GATHER_CHUNK)],
                              sems.at[chunk % 2]).wait()

    @pl.when(c > 0)
    def _():
        wait_chunk(c - 1)

    @pl.when(c == pl.num_programs(0) - 1)
    def _():
        wait_chunk(c)


def _gather_rows(src, idx):
    rows = idx.shape[0]
    grid_spec = pltpu.PrefetchScalarGridSpec(
        num_scalar_prefetch=1,
        grid=(rows // GATHER_CHUNK,),
        in_specs=[pl.BlockSpec(memory_space=pl.ANY)],
        out_specs=pl.BlockSpec(memory_space=pl.ANY),
        scratch_shapes=[pltpu.SemaphoreType.DMA((2,))],
    )
    return pl.pallas_call(
        _gather_kernel,
        grid_spec=grid_spec,
        out_shape=jax.ShapeDtypeStruct((rows,) + src.shape[1:], src.dtype),
        compiler_params=_params("arbitrary"),
        name="gather_rows",
    )(idx, src)


def _expert_kernel(te_ref, nu_ref, x_ref, wg_ref, wu_ref, wo_ref, o_ref, xb_ref, acc_ref):
    i = pl.program_id(0)
    j = pl.program_id(1)
    used = i < nu_ref[0]

    @pl.when(jnp.logical_and(used, j == 0))
    def _():
        xb_ref[...] = _load_row_tiles(x_ref).astype(BF16)

    @pl.when(used)
    def _():
        xb = xb_ref[...]
        act = _silu(_dot(xb, wg_ref[0])) * _dot(xb, wu_ref[0])
        part = _dot(act.astype(BF16), wo_ref[0])

        @pl.when(j == 0)
        def _():
            acc_ref[...] = part

        @pl.when(j > 0)
        def _():
            acc_ref[...] += part

    @pl.when(j == pl.num_programs(1) - 1)
    def _():
        _store_row_tiles(o_ref, acc_ref[...])


def _experts(xs, tile_expert, n_used, w_in, w_out):
    p, nl, _ = xs.shape
    d = nl * LANES
    tm = MOE_TM
    nf = D_EXPERT // MOE_TF
    tiles = pl.BlockSpec((tm, nl, LANES), lambda i, j, te, nu: (i, 0, 0))
    grid_spec = pltpu.PrefetchScalarGridSpec(
        num_scalar_prefetch=2,
        grid=(p // tm, nf),
        in_specs=[
            tiles,
            pl.BlockSpec((1, d, MOE_TF), lambda i, j, te, nu: (te[i], 0, j)),
            pl.BlockSpec((1, d, MOE_TF), lambda i, j, te, nu: (te[i], 0, nf + j)),
            pl.BlockSpec((1, MOE_TF, d), lambda i, j, te, nu: (te[i], j, 0)),
        ],
        out_specs=tiles,
        scratch_shapes=[pltpu.VMEM((tm, d), BF16), pltpu.VMEM((tm, d), F32)],
    )
    return pl.pallas_call(
        _expert_kernel,
        grid_spec=grid_spec,
        out_shape=jax.ShapeDtypeStruct((p, nl, LANES), F32),
        compiler_params=_params("arbitrary", "arbitrary"),
        name="experts",
    )(tile_expert, n_used, xs, w_in, w_in, w_out)


def _combine_ln_kernel(x_ref, y1_ref, y2_ref, g1_ref, g2_ref, g_ref, b_ref, o_ref):
    x, y1, y2 = _load_row_tiles(x_ref), _load_row_tiles(y1_ref), _load_row_tiles(y2_ref)
    y = (DEEPNORM_ALPHA * x + g1_ref[...] * y1) + g2_ref[...] * y2
    o_ref[...] = _layer_norm(y, g_ref[...], b_ref[...])


def _combine_ln(x, y12, g1, g2, g, b, tm):
    n, nl, _ = x.shape
    d = nl * LANES
    tiles = pl.BlockSpec((tm, nl, LANES), lambda i: (i, 0, 0))
    tiles2 = pl.BlockSpec((tm, nl, LANES), lambda i: (n // tm + i, 0, 0))
    col = pl.BlockSpec((tm, 1), lambda i: (i, 0))
    vec = pl.BlockSpec((1, d), lambda i: (0, 0))
    return pl.pallas_call(
        _combine_ln_kernel,
        grid=(n // tm,),
        in_specs=[tiles, tiles, tiles2, col, col, vec, vec],
        out_specs=pl.BlockSpec((tm, d), lambda i: (i, 0)),
        out_shape=jax.ShapeDtypeStruct((n, d), F32),
        compiler_params=_params("parallel"),
        name="combine_ln",
    )(x, y12, y12, g1, g2, g, b)


def _moe(x, w_router, w_in, w_out, g, b):
    n = x.shape[0]
    tm = MOE_TM
    info = _router(x, w_router.T, 1024)
    idx = info[:2].astype(jnp.int32).reshape(2 * n)
    gates = info[2:4]
    onehot = (idx[:, None] == jnp.arange(N_EXPERTS, dtype=jnp.int32)[None, :]).astype(jnp.int32)
    csum = jnp.cumsum(onehot, axis=0)
    rank = jnp.sum(csum * onehot, axis=1) - 1
    counts = csum[-1]
    padded = ((counts + tm - 1) // tm) * tm
    ends = jnp.cumsum(padded)
    starts = ends - padded
    pos = starts[idx] + rank
    p = 2 * n + N_EXPERTS * tm
    token = jnp.arange(2 * n, dtype=jnp.int32) % n
    token_of_pos = jnp.zeros((p,), jnp.int32).at[pos].set(token)
    tile_start = jnp.arange(p // tm, dtype=jnp.int32) * tm
    tile_expert = jnp.minimum(
        jnp.sum((tile_start[:, None] >= ends[None, :]).astype(jnp.int32), axis=1), N_EXPERTS - 1)
    n_used = (ends[-1] // tm).astype(jnp.int32).reshape(1)
    xs = _gather_rows(x, token_of_pos)
    ys = _experts(xs, tile_expert, n_used, w_in, w_out)
    y12 = _gather_rows(ys, pos)
    return _combine_ln(x, y12, gates[0][:, None], gates[1][:, None], g, b, 1024)


def kernel(x, ret_w_in, ret_w_out, sb_w_kv, sb_w_q, sb_w_out, ln_mix_g, ln_mix_b,
           ln_ffn_g, ln_ffn_b, ffn_w_in, ffn_w_out, moe_w_router, moe_w_in, moe_w_out):
    batch, seq, d = x.shape
    n = batch * seq
    x0 = x.reshape(n, d)
    row = lambda v: v.reshape(1, d)

    ret_scale = jnp.concatenate([
        jnp.ones((1, D_MODEL), F32), jnp.full((1, D_MODEL), RET_QK_DIM ** -0.5, F32),
        jnp.ones((1, 2 * RET_V_WIDTH), F32)], axis=1)
    proj = _mm(x0, ret_w_in[0].astype(BF16), ret_scale, 1024, 512, BF16)
    og = _retention(proj, batch, seq)
    x1 = _mm_res_ln(og, ret_w_out[0].astype(BF16), x0, row(ln_mix_g[0]), row(ln_mix_b[0]), 512)
    x2 = _ffn(x1, ffn_w_in[0].astype(BF16), ffn_w_out[0].astype(BF16),
              row(ln_ffn_g[0]), row(ln_ffn_b[0]), 512)

    w_qkv = jnp.concatenate([sb_w_q[0], sb_w_kv], axis=1).astype(BF16)
    sb_scale = jnp.concatenate([
        jnp.full((1, D_MODEL), SB_HEAD_DIM ** -0.5 * LOG2_E, F32),
        jnp.ones((1, 2 * D_MODEL), F32)], axis=1)
    qkv = _mm(x2, w_qkv, sb_scale, 1024, 512, BF16)
    att = _sb_attention(qkv, batch, seq)
    x3 = _mm_res_ln(att, sb_w_out[0].astype(BF16), x2, row(ln_mix_g[1]), row(ln_mix_b[1]), 512,
                    row_tiles=True)
    x4 = _moe(x3, moe_w_router[0], moe_w_in[0].astype(BF16), moe_w_out[0].astype(BF16),
              row(ln_ffn_g[1]), row(ln_ffn_b[1]))
    return x4.reshape(batch, seq, d)
```

```python
import functools
import math

import jax
import jax.numpy as jnp
from jax import lax
from jax.experimental import pallas as pl
from jax.experimental.pallas import tpu as pltpu

D_MODEL = 1024
DEPTH = 2
LANES = 128
NL = D_MODEL // LANES
CHUNK = 64
DEEPNORM_ALPHA = (2.0 * DEPTH) ** 0.25
LN_EPS = 1e-5

RET_HEADS = 4
RET_QK_DIM = D_MODEL // RET_HEADS
RET_V_WIDTH = 2 * D_MODEL
RET_V_DIM = RET_V_WIDTH // RET_HEADS
ROPE_BASE = 10000.0
RET_BLOCK = 256

SB_HEADS = 16
SB_HEAD_DIM = D_MODEL // SB_HEADS
SB_LANES = 256
SB_TILE = 256
SB_ROWS = 256
LOG2_E = math.log2(math.e)
SB_DEAD_LOG2 = 160.0

D_FF = 2816
FFN_TF = 1408
N_EXPERTS = 8
D_EXPERT = 3584
MOE_TF = 512
MOE_TM = 512
TOP_K = 2
GATHER_CHUNK = 1024

VMEM_LIMIT = 56 * 1024 * 1024

BF16 = jnp.bfloat16
F32 = jnp.float32


def _params(*sem):
    return pltpu.CompilerParams(dimension_semantics=sem, vmem_limit_bytes=VMEM_LIMIT)


def _dot(a, b):
    return jnp.dot(a, b, preferred_element_type=F32)


def _dot_nt(a, b):
    return lax.dot_general(a, b, (((1,), (1,)), ((), ())), preferred_element_type=F32)


def _dot_tn(a, b):
    return lax.dot_general(a, b, (((0,), (0,)), ((), ())), preferred_element_type=F32)


def _layer_norm(y, g, b):
    mu = jnp.mean(y, axis=-1, keepdims=True)
    yc = y - mu
    var = jnp.mean(yc * yc, axis=-1, keepdims=True)
    return yc * lax.rsqrt(var + LN_EPS) * g + b


def _silu(x):
    return x * (1.0 / (1.0 + jnp.exp(-x)))


def _store_row_tiles(o_ref, y):
    rows = o_ref.shape[0] // NL
    for s in range(NL):
        o_ref[pl.ds(s, rows, stride=NL), :] = y[:, s * LANES:(s + 1) * LANES].astype(o_ref.dtype)


def _load_row_tiles(x_ref):
    rows = x_ref.shape[0] // NL
    return jnp.concatenate([x_ref[pl.ds(s, rows, stride=NL), :] for s in range(NL)], axis=-1)


def _token_rows(t):
    return pl.ds(pl.multiple_of(t * NL, NL), NL)


def _mm_kernel(a_ref, w_ref, cs_ref, o_ref):
    o_ref[...] = (_dot(a_ref[...].astype(BF16), w_ref[...]) * cs_ref[...]).astype(o_ref.dtype)


def _mm(a, w, col_scale, tm, tn, out_dtype):
    n, k = a.shape
    nout = w.shape[1]
    return pl.pallas_call(
        _mm_kernel,
        grid=(n // tm, nout // tn),
        in_specs=[pl.BlockSpec((tm, k), lambda i, j: (i, 0)),
                  pl.BlockSpec((k, tn), lambda i, j: (0, j)),
                  pl.BlockSpec((1, tn), lambda i, j: (0, j))],
        out_specs=pl.BlockSpec((tm, tn), lambda i, j: (i, j)),
        out_shape=jax.ShapeDtypeStruct((n, nout), out_dtype),
        compiler_params=_params("parallel", "arbitrary"),
        name="proj_mm",
    )(a, w, col_scale)


def _mm_res_ln_kernel(a_ref, w_ref, res_ref, g_ref, b_ref, o_ref):
    y = _dot(a_ref[...].astype(BF16), w_ref[...])
    y = DEEPNORM_ALPHA * res_ref[...] + y
    out = _layer_norm(y, g_ref[...], b_ref[...])
    if o_ref.shape[1] == LANES:
        _store_row_tiles(o_ref, out)
    else:
        o_ref[...] = out


def _mm_res_ln(a, w, res, g, b, tm, row_tiles=False):
    n, k = a.shape
    d = w.shape[1]
    if row_tiles:
        out_spec = pl.BlockSpec((tm * NL, LANES), lambda i: (i, 0))
        out_shape = jax.ShapeDtypeStruct((n * NL, LANES), F32)
    else:
        out_spec = pl.BlockSpec((tm, d), lambda i: (i, 0))
        out_shape = jax.ShapeDtypeStruct((n, d), F32)
    return pl.pallas_call(
        _mm_res_ln_kernel,
        grid=(n // tm,),
        in_specs=[pl.BlockSpec((tm, k), lambda i: (i, 0)),
                  pl.BlockSpec((k, d), lambda i: (0, 0)),
                  pl.BlockSpec((tm, d), lambda i: (i, 0)),
                  pl.BlockSpec((1, d), lambda i: (0, 0)),
                  pl.BlockSpec((1, d), lambda i: (0, 0))],
        out_specs=out_spec,
        out_shape=out_shape,
        compiler_params=_params("parallel"),
        name="mm_res_ln",
    )(a, w, res, g, b)


def _rotary(x, cos, sin):
    half = x.shape[-1] // 2
    x1, x2 = x[:, :half], x[:, half:]
    return jnp.concatenate([x1 * cos - x2 * sin, x1 * sin + x2 * cos], axis=-1)


def _retention_kernel(q_ref, k_ref, v_ref, g_ref, cos_ref, sin_ref, dmat_ref,
                      qd_ref, kd_ref, cd_ref, o_ref, state_ref):
    @pl.when(pl.program_id(2) == 0)
    def _():
        state_ref[...] = jnp.zeros_like(state_ref)

    cos, sin = cos_ref[...], sin_ref[...]
    q = _rotary(q_ref[...].astype(F32), cos, sin)
    k = _rotary(k_ref[...].astype(F32), cos, sin)
    v = v_ref[...]
    qb = q.astype(BF16)
    state = state_ref[...]
    scores = _dot_nt(qb, k.astype(BF16)) * dmat_ref[0]
    o = _dot(scores.astype(BF16), v) + qd_ref[0] * _dot(qb, state.astype(BF16))
    state_ref[...] = state * cd_ref[0] + _dot_tn((k * kd_ref[0]).astype(BF16), v)
    mu = jnp.mean(o, axis=-1, keepdims=True)
    oc = o - mu
    var = jnp.mean(oc * oc, axis=-1, keepdims=True)
    on = oc * lax.rsqrt(var + LN_EPS)
    o_ref[...] = (_silu(g_ref[...].astype(F32)) * on).astype(o_ref.dtype)


def _retention_tables(seq):
    t = RET_BLOCK
    half = RET_QK_DIM // 2
    inv_freq = 1.0 / (ROPE_BASE ** (jnp.arange(0, RET_QK_DIM, 2, dtype=F32) / RET_QK_DIM))
    ang = jnp.arange(seq, dtype=F32)[:, None] * inv_freq[None, :]
    cos, sin = jnp.cos(ang), jnp.sin(ang)
    assert cos.shape == (seq, half)
    log_g = jnp.log(1.0 - 2.0 ** (-5.0 - jnp.arange(RET_HEADS, dtype=F32)))
    idx = jnp.arange(t, dtype=F32)
    diff = idx[:, None] - idx[None, :]
    chunk = jnp.arange(t) // CHUNK
    same = chunk[:, None] == chunk[None, :]
    earlier = chunk[None, :] < chunk[:, None]
    lg = log_g[:, None, None]
    dmat = jnp.where(same[None], jnp.exp(lg * jnp.abs(diff)[None]),
                     jnp.where(earlier[None], jnp.exp(lg * diff[None]), 0.0))
    qd = jnp.exp(log_g[:, None] * (idx + 1.0))[..., None]
    kd = jnp.exp(log_g[:, None] * (t - 1.0 - idx))[..., None]
    cd = jnp.broadcast_to(jnp.exp(log_g * t)[:, None, None], (RET_HEADS, 1, RET_V_DIM))
    return cos, sin, dmat, qd, kd, cd


def _retention(proj, batch, seq):
    n = proj.shape[0]
    t = RET_BLOCK
    nt = seq // t
    cos, sin, dmat, qd, kd, cd = _retention_tables(seq)
    qk_blocks = D_MODEL // RET_QK_DIM
    v_off = 2 * D_MODEL // RET_V_DIM
    g_off = v_off + RET_V_WIDTH // RET_V_DIM
    half = RET_QK_DIM // 2
    row = lambda b, h, i: b * nt + i
    return pl.pallas_call(
        _retention_kernel,
        grid=(batch, RET_HEADS, nt),
        in_specs=[
            pl.BlockSpec((t, RET_QK_DIM), lambda b, h, i: (row(b, h, i), h)),
            pl.BlockSpec((t, RET_QK_DIM), lambda b, h, i: (row(b, h, i), qk_blocks + h)),
            pl.BlockSpec((t, RET_V_DIM), lambda b, h, i: (row(b, h, i), v_off + h)),
            pl.BlockSpec((t, RET_V_DIM), lambda b, h, i: (row(b, h, i), g_off + h)),
            pl.BlockSpec((t, half), lambda b, h, i: (i, 0)),
            pl.BlockSpec((t, half), lambda b, h, i: (i, 0)),
            pl.BlockSpec((1, t, t), lambda b, h, i: (h, 0, 0)),
            pl.BlockSpec((1, t, 1), lambda b, h, i: (h, 0, 0)),
            pl.BlockSpec((1, t, 1), lambda b, h, i: (h, 0, 0)),
            pl.BlockSpec((1, 1, RET_V_DIM), lambda b, h, i: (h, 0, 0)),
        ],
        out_specs=pl.BlockSpec((t, RET_V_DIM), lambda b, h, i: (row(b, h, i), h)),
        out_shape=jax.ShapeDtypeStruct((n, RET_V_WIDTH), BF16),
        scratch_shapes=[pltpu.VMEM((RET_QK_DIM, RET_V_DIM), F32)],
        compiler_params=_params("parallel", "parallel", "arbitrary"),
        name="retention",
    )(proj, proj, proj, proj, cos, sin, dmat, qd, kd, cd)


def _ffn_kernel(x_ref, wg_ref, wu_ref, wo_ref, g_ref, b_ref, o_ref, acc_ref):
    j = pl.program_id(1)
    xb = x_ref[...].astype(BF16)
    act = _silu(_dot(xb, wg_ref[...])) * _dot(xb, wu_ref[...])
    part = _dot(act.astype(BF16), wo_ref[...])

    @pl.when(j == 0)
    def _():
        acc_ref[...] = part

    @pl.when(j > 0)
    def _():
        acc_ref[...] += part

    @pl.when(j == pl.num_programs(1) - 1)
    def _():
        y = DEEPNORM_ALPHA * x_ref[...] + acc_ref[...]
        o_ref[...] = _layer_norm(y, g_ref[...], b_ref[...])


def _ffn(x, w_in, w_out, g, b, tm):
    n, d = x.shape
    nf = D_FF // FFN_TF
    return pl.pallas_call(
        _ffn_kernel,
        grid=(n // tm, nf),
        in_specs=[pl.BlockSpec((tm, d), lambda i, j: (i, 0)),
                  pl.BlockSpec((d, FFN_TF), lambda i, j: (0, j)),
                  pl.BlockSpec((d, FFN_TF), lambda i, j: (0, nf + j)),
                  pl.BlockSpec((FFN_TF, d), lambda i, j: (j, 0)),
                  pl.BlockSpec((1, d), lambda i, j: (0, 0)),
                  pl.BlockSpec((1, d), lambda i, j: (0, 0))],
        out_specs=pl.BlockSpec((tm, d), lambda i, j: (i, 0)),
        out_shape=jax.ShapeDtypeStruct((n, d), F32),
        scratch_shapes=[pltpu.VMEM((tm, d), F32)],
        compiler_params=_params("parallel", "arbitrary"),
        name="ffn",
    )(x, w_in, w_in, w_out, g, b)


def _softplus2(z):
    neg_abs = lax.bitcast_convert_type(
        lax.bitcast_convert_type(z, jnp.uint32) | jnp.uint32(0x80000000), F32)
    return jnp.maximum(z, 0.0) + jnp.log(1.0 + jnp.exp2(neg_abs)) * LOG2_E


def _sb_kernel(q_ref, k_ref, v_ref, o_ref, vm_ref):
    i = pl.program_id(2)
    t = SB_TILE
    r = SB_ROWS
    nh = SB_LANES // SB_HEAD_DIM
    lane = lax.broadcasted_iota(jnp.int32, (1, SB_LANES), 1)
    in_head = [(lane >= h * SB_HEAD_DIM) & (lane < (h + 1) * SB_HEAD_DIM) for h in range(nh)]

    @pl.when(i == 0)
    def _():
        v_all = v_ref[...]
        for h in range(nh):
            vm_ref[h] = jnp.where(in_head[h], v_all, jnp.zeros_like(v_all))

    q = q_ref[...]
    qm = [jnp.where(in_head[h], q, jnp.zeros_like(q)) for h in range(nh)]
    row = lax.broadcasted_iota(jnp.int32, (t, t), 0)
    col = lax.broadcasted_iota(jnp.int32, (t, t), 1)
    neg_suffix = jnp.where(row > col, -1.0, 0.0).astype(BF16)
    causal = col < row

    def tile(j, runs, accs, diag):
        start = pl.multiple_of(j * t, t)
        kt = k_ref[pl.ds(start, t), :]
        new_runs = [[None] * (t // r) for _ in range(nh)]
        new_accs = list(accs)
        for s in range(t // r):
            rows = slice(s * r, (s + 1) * r)
            zs = [_dot_nt(qm[h][rows], kt) for h in range(nh)]
            sps = [_softplus2(z) for z in zs]
            spms = [jnp.where(causal[rows], sp, 0.0) if diag else sp for sp in sps]
            betweens = [_dot(spms[h].astype(BF16), neg_suffix) - runs[h][s] for h in range(nh)]
            ws = [jnp.exp2(zs[h] - sps[h] + betweens[h]) for h in range(nh)]
            if diag:
                ws = [jnp.where(causal[rows], w, 0.0) for w in ws]
            pv = [_dot(ws[h].astype(BF16), vm_ref[h, pl.ds(start, t), :]) for h in range(nh)]
            new_accs[s] = new_accs[s] + ((pv[0] + pv[1]) + (pv[2] + pv[3]))
            for h in range(nh):
                new_runs[h][s] = runs[h][s] + jnp.sum(spms[h], axis=1, keepdims=True)
        return new_runs, new_accs

    runs0 = [[jnp.zeros((r, 1), F32) for _ in range(t // r)] for _ in range(nh)]
    accs0 = [jnp.zeros((r, SB_LANES), F32) for _ in range(t // r)]
    runs, accs = tile(i, runs0, accs0, True)

    def all_dead(runs):
        m = functools.reduce(jnp.minimum, [x for per_head in runs for x in per_head])
        return (jnp.min(m) > SB_DEAD_LOG2).astype(jnp.int32)

    def cond(c):
        return jnp.logical_and(c[0] >= 0, c[1] == 0)

    def body(c):
        j, _, runs, accs = c
        runs, accs = tile(j, runs, accs, False)
        return j - 1, all_dead(runs), runs, accs

    _, _, runs, accs = lax.while_loop(cond, body, (i - 1, all_dead(runs), runs, accs))
    for s in range(t // r):
        o_ref[s * r:(s + 1) * r, :] = accs[s].astype(o_ref.dtype)


def _sb_attention(qkv, batch, seq):
    n = qkv.shape[0]
    t = SB_TILE
    nt = seq // t
    ng = D_MODEL // SB_LANES
    nh = SB_LANES // SB_HEAD_DIM
    return pl.pallas_call(
        _sb_kernel,
        grid=(batch, ng, nt),
        in_specs=[
            pl.BlockSpec((t, SB_LANES), lambda b, g, i: (b * nt + i, g)),
            pl.BlockSpec((seq, SB_LANES), lambda b, g, i: (b, ng + g)),
            pl.BlockSpec((seq, SB_LANES), lambda b, g, i: (b, 2 * ng + g)),
        ],
        out_specs=pl.BlockSpec((t, SB_LANES), lambda b, g, i: (b * nt + i, g)),
        out_shape=jax.ShapeDtypeStruct((n, D_MODEL), BF16),
        scratch_shapes=[pltpu.VMEM((nh, seq, SB_LANES), BF16)],
        compiler_params=_params("arbitrary", "arbitrary", "arbitrary"),
        name="sb_attention",
    )(qkv, qkv, qkv)


def _router_kernel(x_ref, wr_ref, o_ref):
    logits = lax.dot_general(wr_ref[...], _load_row_tiles(x_ref), (((1,), (1,)), ((), ())),
                             precision=lax.Precision.HIGHEST,
                             preferred_element_type=F32)
    e_idx = lax.broadcasted_iota(jnp.int32, logits.shape, 0)
    m1 = jnp.max(logits, axis=0, keepdims=True)
    i1 = jnp.min(jnp.where(logits == m1, e_idx, N_EXPERTS), axis=0, keepdims=True)
    rest = jnp.where(e_idx == i1, -jnp.inf, logits)
    m2 = jnp.max(rest, axis=0, keepdims=True)
    i2 = jnp.min(jnp.where(rest == m2, e_idx, N_EXPERTS), axis=0, keepdims=True)
    e2 = jnp.exp(m2 - m1)
    den = 1.0 + e2
    g1 = 1.0 / den
    g2 = e2 / den
    zeros = jnp.zeros((N_EXPERTS - 4, logits.shape[1]), F32)
    o_ref[...] = jnp.concatenate([i1.astype(F32), i2.astype(F32), g1, g2, zeros], axis=0)


def _router(x, w_router_t, tm):
    n = x.shape[0] // NL
    return pl.pallas_call(
        _router_kernel,
        grid=(n // tm,),
        in_specs=[pl.BlockSpec((tm * NL, LANES), lambda i: (i, 0)),
                  pl.BlockSpec((N_EXPERTS, D_MODEL), lambda i: (0, 0))],
        out_specs=pl.BlockSpec((N_EXPERTS, tm), lambda i: (0, i)),
        out_shape=jax.ShapeDtypeStruct((N_EXPERTS, n), F32),
        compiler_params=_params("parallel"),
        name="router",
    )(x, w_router_t)


def _start_row_gather(idx_ref, src_ref, dst_ref, sem, base):
    def start(r, carry):
        pltpu.make_async_copy(src_ref.at[_token_rows(idx_ref[base + r])],
                              dst_ref.at[_token_rows(r)], sem).start()
        return carry

    lax.fori_loop(0, dst_ref.shape[0] // NL, start, 0, unroll=8)


def _wait_row_gather(src_ref, dst_ref, sem):
    pltpu.make_async_copy(src_ref.at[pl.ds(0, dst_ref.shape[0])], dst_ref, sem).wait()


def _dispatch_kernel(pos_ref, pad_start_ref, pad_count_ref, x_ref, out_ref, zero_ref, sems):
    i = pl.program_id(0)
    tm = x_ref.shape[0] // NL
    n = pl.num_programs(0) * tm
    fill_sem = sems.at[TOP_K]

    @pl.when(i == 0)
    def _():
        zero_ref[...] = jnp.zeros_like(zero_ref)
        for e in range(N_EXPERTS):
            def fill(r, carry, e=e):
                pltpu.make_async_copy(zero_ref, out_ref.at[_token_rows(pad_start_ref[e] + r)],
                                      fill_sem).start()
                return carry

            def drain(r, carry):
                pltpu.make_async_copy(zero_ref, out_ref.at[_token_rows(0)], fill_sem).wait()
                return carry

            lax.fori_loop(0, pad_count_ref[e], fill, 0)
            lax.fori_loop(0, pad_count_ref[e], drain, 0)

    for c in range(TOP_K):
        def start(r, carry, c=c):
            pltpu.make_async_copy(x_ref.at[_token_rows(r)],
                                  out_ref.at[_token_rows(pos_ref[c * n + i * tm + r])],
                                  sems.at[c]).start()
            return carry

        lax.fori_loop(0, tm, start, 0, unroll=8)
    for c in range(TOP_K):
        pltpu.make_async_copy(x_ref, out_ref.at[pl.ds(0, tm * NL)], sems.at[c]).wait()


def _dispatch_rows(x, pos, pad_start, pad_count, tokens):
    n = x.shape[0] // NL
    tm = GATHER_CHUNK
    grid_spec = pltpu.PrefetchScalarGridSpec(
        num_scalar_prefetch=3,
        grid=(n // tm,),
        in_specs=[pl.BlockSpec((tm * NL, LANES), lambda i, *_: (i, 0))],
        out_specs=pl.BlockSpec(memory_space=pl.ANY),
        scratch_shapes=[pltpu.VMEM((NL, LANES), x.dtype),
                        pltpu.SemaphoreType.DMA((TOP_K + 1,))],
    )
    return pl.pallas_call(
        _dispatch_kernel,
        grid_spec=grid_spec,
        out_shape=jax.ShapeDtypeStruct((tokens * NL, LANES), x.dtype),
        compiler_params=_params("arbitrary"),
        name="dispatch_rows",
    )(pos, pad_start, pad_count, x)


def _expert_kernel(te_ref, nu_ref, x_ref, wg_ref, wu_ref, wo_ref, o_ref, xb_ref, acc_ref):
    i = pl.program_id(0)
    j = pl.program_id(1)
    used = i < nu_ref[0]

    @pl.when(jnp.logical_and(used, j == 0))
    def _():
        xb_ref[...] = _load_row_tiles(x_ref).astype(BF16)

    @pl.when(used)
    def _():
        xb = xb_ref[...]
        act = _silu(_dot(xb, wg_ref[0])) * _dot(xb, wu_ref[0])
        part = _dot(act.astype(BF16), wo_ref[0])

        @pl.when(j == 0)
        def _():
            acc_ref[...] = part

        @pl.when(j > 0)
        def _():
            acc_ref[...] += part

    @pl.when(j == pl.num_programs(1) - 1)
    def _():
        _store_row_tiles(o_ref, acc_ref[...])


def _experts(xs, tile_expert, n_used, w_in, w_out):
    p = xs.shape[0] // NL
    d = D_MODEL
    tm = MOE_TM
    nf = D_EXPERT // MOE_TF
    tiles = pl.BlockSpec((tm * NL, LANES), lambda i, j, te, nu: (i, 0))
    grid_spec = pltpu.PrefetchScalarGridSpec(
        num_scalar_prefetch=2,
        grid=(p // tm, nf),
        in_specs=[
            tiles,
            pl.BlockSpec((1, d, MOE_TF), lambda i, j, te, nu: (te[i], 0, j)),
            pl.BlockSpec((1, d, MOE_TF), lambda i, j, te, nu: (te[i], 0, nf + j)),
            pl.BlockSpec((1, MOE_TF, d), lambda i, j, te, nu: (te[i], j, 0)),
        ],
        out_specs=tiles,
        scratch_shapes=[pltpu.VMEM((tm, d), BF16), pltpu.VMEM((tm, d), F32)],
    )
    return pl.pallas_call(
        _expert_kernel,
        grid_spec=grid_spec,
        out_shape=jax.ShapeDtypeStruct((p * NL, LANES), F32),
        compiler_params=_params("arbitrary", "arbitrary"),
        name="experts",
    )(tile_expert, n_used, xs, w_in, w_in, w_out)


def _combine_ln_kernel(pos_ref, x_ref, ys_ref, g1_ref, g2_ref, g_ref, b_ref, o_ref, ybuf, sems):
    i = pl.program_id(0)
    steps = pl.num_programs(0)
    tm = x_ref.shape[0] // NL
    n = steps * tm

    def start(step, slot):
        for c in range(TOP_K):
            _start_row_gather(pos_ref, ys_ref, ybuf.at[slot, c], sems.at[slot, c],
                              c * n + step * tm)

    @pl.when(i == 0)
    def _():
        start(0, 0)

    @pl.when(i + 1 < steps)
    def _():
        start(i + 1, (i + 1) % 2)

    slot = i % 2
    for c in range(TOP_K):
        _wait_row_gather(ys_ref, ybuf.at[slot, c], sems.at[slot, c])
    y1, y2 = _load_row_tiles(ybuf.at[slot, 0]), _load_row_tiles(ybuf.at[slot, 1])
    y = (DEEPNORM_ALPHA * _load_row_tiles(x_ref) + g1_ref[...] * y1) + g2_ref[...] * y2
    o_ref[...] = _layer_norm(y, g_ref[...], b_ref[...])


def _combine_ln(x, ys, pos, g1, g2, g, b, tm):
    n = x.shape[0] // NL
    d = D_MODEL
    grid_spec = pltpu.PrefetchScalarGridSpec(
        num_scalar_prefetch=1,
        grid=(n // tm,),
        in_specs=[pl.BlockSpec((tm * NL, LANES), lambda i, pos: (i, 0)),
                  pl.BlockSpec(memory_space=pl.ANY),
                  pl.BlockSpec((tm, 1), lambda i, pos: (i, 0)),
                  pl.BlockSpec((tm, 1), lambda i, pos: (i, 0)),
                  pl.BlockSpec((1, d), lambda i, pos: (0, 0)),
                  pl.BlockSpec((1, d), lambda i, pos: (0, 0))],
        out_specs=pl.BlockSpec((tm, d), lambda i, pos: (i, 0)),
        scratch_shapes=[pltpu.VMEM((2, TOP_K, tm * NL, LANES), F32),
                        pltpu.SemaphoreType.DMA((2, TOP_K))],
    )
    return pl.pallas_call(
        _combine_ln_kernel,
        grid_spec=grid_spec,
        out_shape=jax.ShapeDtypeStruct((n, d), F32),
        compiler_params=_params("arbitrary"),
        name="combine_ln",
    )(pos, x, ys, g1, g2, g, b)


def _moe(x, w_router, w_in, w_out, g, b):
    n = x.shape[0] // NL
    tm = MOE_TM
    info = _router(x, w_router.T, 1024)
    idx = info[:2].astype(jnp.int32).reshape(2 * n)
    gates = info[2:4]
    onehot = (idx[:, None] == jnp.arange(N_EXPERTS, dtype=jnp.int32)[None, :]).astype(jnp.int32)
    csum = jnp.cumsum(onehot, axis=0)
    rank = jnp.sum(csum * onehot, axis=1) - 1
    counts = csum[-1]
    padded = ((counts + tm - 1) // tm) * tm
    ends = jnp.cumsum(padded)
    starts = ends - padded
    pos = starts[idx] + rank
    p = 2 * n + N_EXPERTS * tm
    tile_start = jnp.arange(p // tm, dtype=jnp.int32) * tm
    tile_expert = jnp.minimum(
        jnp.sum((tile_start[:, None] >= ends[None, :]).astype(jnp.int32), axis=1), N_EXPERTS - 1)
    n_used = (ends[-1] // tm).astype(jnp.int32).reshape(1)
    pad_count = (padded - counts).at[N_EXPERTS - 1].add(p - ends[-1])
    xs = _dispatch_rows(x, pos, starts + counts, pad_count, p)
    ys = _experts(xs, tile_expert, n_used, w_in, w_out)
    return _combine_ln(x, ys, pos, gates[0][:, None], gates[1][:, None], g, b, 512)


def kernel(x, ret_w_in, ret_w_out, sb_w_kv, sb_w_q, sb_w_out, ln_mix_g, ln_mix_b,
           ln_ffn_g, ln_ffn_b, ffn_w_in, ffn_w_out, moe_w_router, moe_w_in, moe_w_out):
    batch, seq, d = x.shape
    n = batch * seq
    x0 = x.reshape(n, d)
    row = lambda v: v.reshape(1, d)

    ret_scale = jnp.concatenate([
        jnp.ones((1, D_MODEL), F32), jnp.full((1, D_MODEL), RET_QK_DIM ** -0.5, F32),
        jnp.ones((1, 2 * RET_V_WIDTH), F32)], axis=1)
    proj = _mm(x0, ret_w_in[0].astype(BF16), ret_scale, 1024, 512, BF16)
    og = _retention(proj, batch, seq)
    x1 = _mm_res_ln(og, ret_w_out[0].astype(BF16), x0, row(ln_mix_g[0]), row(ln_mix_b[0]), 512)
    x2 = _ffn(x1, ffn_w_in[0].astype(BF16), ffn_w_out[0].astype(BF16),
              row(ln_ffn_g[0]), row(ln_ffn_b[0]), 512)

    w_qkv = jnp.concatenate([sb_w_q[0], sb_w_kv], axis=1).astype(BF16)
    sb_scale = jnp.concatenate([
        jnp.full((1, D_MODEL), SB_HEAD_DIM ** -0.5 * LOG2_E, F32),
        jnp.ones((1, 2 * D_MODEL), F32)], axis=1)
    qkv = _mm(x2, w_qkv, sb_scale, 1024, 512, BF16)
    att = _sb_attention(qkv, batch, seq)
    x3 = _mm_res_ln(att, sb_w_out[0].astype(BF16), x2, row(ln_mix_g[1]), row(ln_mix_b[1]), 512,
                    row_tiles=True)
    x4 = _moe(x3, moe_w_router[0], moe_w_in[0].astype(BF16), moe_w_out[0].astype(BF16),
              row(ln_ffn_g[1]), row(ln_ffn_b[1]))
    return x4.reshape(batch, seq, d)
```

```python
import functools
import math

import jax
import jax.numpy as jnp
from jax import lax
from jax.experimental import pallas as pl
from jax.experimental.pallas import tpu as pltpu

D_MODEL = 1024
DEPTH = 2
LANES = 128
NL = D_MODEL // LANES
CHUNK = 64
DEEPNORM_ALPHA = (2.0 * DEPTH) ** 0.25
LN_EPS = 1e-5

RET_HEADS = 4
RET_QK_DIM = D_MODEL // RET_HEADS
RET_V_WIDTH = 2 * D_MODEL
RET_V_DIM = RET_V_WIDTH // RET_HEADS
ROPE_BASE = 10000.0
RET_BLOCK = 256

SB_HEADS = 16
SB_HEAD_DIM = D_MODEL // SB_HEADS
SB_LANES = 256
SB_TILE = 256
SB_ROWS = 256
LOG2_E = math.log2(math.e)
SB_DEAD_LOG2 = 160.0

D_FF = 2816
FFN_TF = 1408
N_EXPERTS = 8
D_EXPERT = 3584
MOE_TF = 1792
MOE_STAGE_ROWS_IN = 32
MOE_STAGE_ROWS_OUT = 512
MOE_TM = 512
TOP_K = 2
GATHER_CHUNK = 1024

VMEM_LIMIT = 56 * 1024 * 1024

BF16 = jnp.bfloat16
F32 = jnp.float32


def _params(*sem):
    return pltpu.CompilerParams(dimension_semantics=sem, vmem_limit_bytes=VMEM_LIMIT)


def _dot(a, b):
    return jnp.dot(a, b, preferred_element_type=F32)


def _dot_nt(a, b):
    return lax.dot_general(a, b, (((1,), (1,)), ((), ())), preferred_element_type=F32)


def _dot_tn(a, b):
    return lax.dot_general(a, b, (((0,), (0,)), ((), ())), preferred_element_type=F32)


def _layer_norm(y, g, b):
    mu = jnp.mean(y, axis=-1, keepdims=True)
    yc = y - mu
    var = jnp.mean(yc * yc, axis=-1, keepdims=True)
    return yc * lax.rsqrt(var + LN_EPS) * g + b


def _silu(x):
    return x * (1.0 / (1.0 + jnp.exp(-x)))


def _store_row_tiles(o_ref, y):
    rows = o_ref.shape[0] // NL
    for s in range(NL):
        o_ref[pl.ds(s, rows, stride=NL), :] = y[:, s * LANES:(s + 1) * LANES].astype(o_ref.dtype)


def _load_row_tiles(x_ref):
    rows = x_ref.shape[0] // NL
    return jnp.concatenate([x_ref[pl.ds(s, rows, stride=NL), :] for s in range(NL)], axis=-1)


def _token_rows(t):
    return pl.ds(pl.multiple_of(t * NL, NL), NL)


def _resident(shape):
    return pl.BlockSpec(shape, lambda *_: (0,) * len(shape), pipeline_mode=pl.Buffered(1))


def _mm_kernel(a_ref, w_ref, cs_ref, o_ref, *, tn):
    ab = a_ref[...].astype(BF16)
    for c in range(w_ref.shape[1] // tn):
        cols = slice(c * tn, (c + 1) * tn)
        o_ref[:, cols] = (_dot(ab, w_ref[:, cols]) * cs_ref[:, cols]).astype(o_ref.dtype)


def _mm(a, w, col_scale, tm, tn, out_dtype):
    n, k = a.shape
    nout = w.shape[1]
    return pl.pallas_call(
        functools.partial(_mm_kernel, tn=tn),
        grid=(n // tm,),
        in_specs=[pl.BlockSpec((tm, k), lambda i: (i, 0)),
                  _resident((k, nout)),
                  _resident((1, nout))],
        out_specs=pl.BlockSpec((tm, nout), lambda i: (i, 0)),
        out_shape=jax.ShapeDtypeStruct((n, nout), out_dtype),
        compiler_params=_params("parallel"),
        name="proj_mm",
    )(a, w, col_scale)


def _mm_res_ln_kernel(a_ref, w_ref, res_ref, g_ref, b_ref, o_ref):
    half = a_ref.shape[0] // 2
    for h in range(2):
        rows = slice(h * half, (h + 1) * half)
        y = _dot(a_ref[rows, :].astype(BF16), w_ref[...])
        y = DEEPNORM_ALPHA * res_ref[rows, :] + y
        out = _layer_norm(y, g_ref[...], b_ref[...])
        if o_ref.shape[1] == LANES:
            _store_row_tiles(o_ref.at[pl.ds(h * half * NL, half * NL)], out)
        else:
            o_ref[rows, :] = out


def _mm_res_ln(a, w, res, g, b, tm, row_tiles=False):
    n, k = a.shape
    d = w.shape[1]
    if row_tiles:
        out_spec = pl.BlockSpec((tm * NL, LANES), lambda i: (i, 0))
        out_shape = jax.ShapeDtypeStruct((n * NL, LANES), F32)
    else:
        out_spec = pl.BlockSpec((tm, d), lambda i: (i, 0))
        out_shape = jax.ShapeDtypeStruct((n, d), F32)
    return pl.pallas_call(
        _mm_res_ln_kernel,
        grid=(n // tm,),
        in_specs=[pl.BlockSpec((tm, k), lambda i: (i, 0)),
                  _resident((k, d)),
                  pl.BlockSpec((tm, d), lambda i: (i, 0)),
                  _resident((1, d)),
                  _resident((1, d))],
        out_specs=out_spec,
        out_shape=out_shape,
        compiler_params=_params("parallel"),
        name="mm_res_ln",
    )(a, w, res, g, b)


def _rotary(x, cos, sin):
    half = x.shape[-1] // 2
    x1, x2 = x[:, :half], x[:, half:]
    return jnp.concatenate([x1 * cos - x2 * sin, x1 * sin + x2 * cos], axis=-1)


def _retention_kernel(q_ref, k_ref, v_ref, g_ref, cos_ref, sin_ref, dmat_ref,
                      qd_ref, kd_ref, cd_ref, o_ref, state_ref):
    @pl.when(pl.program_id(1) == 0)
    def _():
        state_ref[...] = jnp.zeros_like(state_ref)

    cos, sin = cos_ref[...], sin_ref[...]
    for h in range(RET_HEADS):
        qk_cols = slice(h * RET_QK_DIM, (h + 1) * RET_QK_DIM)
        v_cols = slice(h * RET_V_DIM, (h + 1) * RET_V_DIM)
        q = _rotary(q_ref[:, qk_cols].astype(F32), cos, sin)
        k = _rotary(k_ref[:, qk_cols].astype(F32), cos, sin)
        v = v_ref[:, v_cols]
        qb = q.astype(BF16)
        state = state_ref[h]
        scores = _dot_nt(qb, k.astype(BF16)) * dmat_ref[h]
        o = _dot(scores.astype(BF16), v) + qd_ref[h] * _dot(qb, state.astype(BF16))
        state_ref[h] = state * cd_ref[h] + _dot_tn((k * kd_ref[h]).astype(BF16), v)
        mu = jnp.mean(o, axis=-1, keepdims=True)
        oc = o - mu
        var = jnp.mean(oc * oc, axis=-1, keepdims=True)
        on = oc * lax.rsqrt(var + LN_EPS)
        o_ref[:, v_cols] = (_silu(g_ref[:, v_cols].astype(F32)) * on).astype(o_ref.dtype)


def _retention_tables(seq):
    t = RET_BLOCK
    half = RET_QK_DIM // 2
    inv_freq = 1.0 / (ROPE_BASE ** (jnp.arange(0, RET_QK_DIM, 2, dtype=F32) / RET_QK_DIM))
    ang = jnp.arange(seq, dtype=F32)[:, None] * inv_freq[None, :]
    cos, sin = jnp.cos(ang), jnp.sin(ang)
    assert cos.shape == (seq, half)
    log_g = jnp.log(1.0 - 2.0 ** (-5.0 - jnp.arange(RET_HEADS, dtype=F32)))
    idx = jnp.arange(t, dtype=F32)
    diff = idx[:, None] - idx[None, :]
    chunk = jnp.arange(t) // CHUNK
    same = chunk[:, None] == chunk[None, :]
    earlier = chunk[None, :] < chunk[:, None]
    lg = log_g[:, None, None]
    dmat = jnp.where(same[None], jnp.exp(lg * jnp.abs(diff)[None]),
                     jnp.where(earlier[None], jnp.exp(lg * diff[None]), 0.0))
    qd = jnp.exp(log_g[:, None] * (idx + 1.0))[..., None]
    kd = jnp.exp(log_g[:, None] * (t - 1.0 - idx))[..., None]
    cd = jnp.broadcast_to(jnp.exp(log_g * t)[:, None, None], (RET_HEADS, 1, RET_V_DIM))
    return cos, sin, dmat, qd, kd, cd


def _retention(proj, batch, seq):
    n = proj.shape[0]
    t = RET_BLOCK
    nt = seq // t
    cos, sin, dmat, qd, kd, cd = _retention_tables(seq)
    half = RET_QK_DIM // 2
    return pl.pallas_call(
        _retention_kernel,
        grid=(batch, nt),
        in_specs=[
            pl.BlockSpec((t, D_MODEL), lambda b, i: (b * nt + i, 0)),
            pl.BlockSpec((t, D_MODEL), lambda b, i: (b * nt + i, 1)),
            pl.BlockSpec((t, RET_V_WIDTH), lambda b, i: (b * nt + i, 1)),
            pl.BlockSpec((t, RET_V_WIDTH), lambda b, i: (b * nt + i, 2)),
            pl.BlockSpec((t, half), lambda b, i: (i, 0)),
            pl.BlockSpec((t, half), lambda b, i: (i, 0)),
            _resident((RET_HEADS, t, t)),
            _resident((RET_HEADS, t, 1)),
            _resident((RET_HEADS, t, 1)),
            _resident((RET_HEADS, 1, RET_V_DIM)),
        ],
        out_specs=pl.BlockSpec((t, RET_V_WIDTH), lambda b, i: (b * nt + i, 0)),
        out_shape=jax.ShapeDtypeStruct((n, RET_V_WIDTH), BF16),
        scratch_shapes=[pltpu.VMEM((RET_HEADS, RET_QK_DIM, RET_V_DIM), F32)],
        compiler_params=_params("parallel", "arbitrary"),
        name="retention",
    )(proj, proj, proj, proj, cos, sin, dmat, qd, kd, cd)


def _ffn_kernel(x_ref, wi_ref, wo_ref, g_ref, b_ref, o_ref):
    x = x_ref[...]
    xb = x.astype(BF16)
    y = DEEPNORM_ALPHA * x
    for c in range(D_FF // FFN_TF):
        gate = _dot(xb, wi_ref[:, c * FFN_TF:(c + 1) * FFN_TF])
        up = _dot(xb, wi_ref[:, D_FF + c * FFN_TF:D_FF + (c + 1) * FFN_TF])
        act = (_silu(gate) * up).astype(BF16)
        y = y + _dot(act, wo_ref[c * FFN_TF:(c + 1) * FFN_TF, :])
    o_ref[...] = _layer_norm(y, g_ref[...], b_ref[...])


def _ffn(x, w_in, w_out, g, b, tm):
    n, d = x.shape
    return pl.pallas_call(
        _ffn_kernel,
        grid=(n // tm,),
        in_specs=[pl.BlockSpec((tm, d), lambda i: (i, 0)),
                  _resident((d, 2 * D_FF)),
                  _resident((D_FF, d)),
                  _resident((1, d)),
                  _resident((1, d))],
        out_specs=pl.BlockSpec((tm, d), lambda i: (i, 0)),
        out_shape=jax.ShapeDtypeStruct((n, d), F32),
        compiler_params=_params("parallel"),
        name="ffn",
    )(x, w_in, w_out, g, b)


def _softplus2(z):
    neg_abs = lax.bitcast_convert_type(
        lax.bitcast_convert_type(z, jnp.uint32) | jnp.uint32(0x80000000), F32)
    return jnp.maximum(z, 0.0) + jnp.log(1.0 + jnp.exp2(neg_abs)) * LOG2_E


def _sb_kernel(q_ref, k_ref, v_ref, o_ref, vm_ref):
    i = pl.program_id(2)
    t = SB_TILE
    nh = SB_LANES // SB_HEAD_DIM
    lane = lax.broadcasted_iota(jnp.int32, (1, SB_LANES), 1)
    in_head = [(lane >= h * SB_HEAD_DIM) & (lane < (h + 1) * SB_HEAD_DIM) for h in range(nh)]

    @pl.when(i == 0)
    def _():
        for j in range(vm_ref.shape[0]):
            vt = v_ref[j * t:(j + 1) * t, :]
            for h in range(nh):
                vm_ref[j, h * t:(h + 1) * t, :] = jnp.where(in_head[h], vt, jnp.zeros_like(vt))

    q = q_ref[...]
    qs = jnp.concatenate([jnp.where(in_head[h], q, jnp.zeros_like(q)) for h in range(nh)], axis=0)
    row = lax.broadcasted_iota(jnp.int32, (t, t), 0)
    col = lax.broadcasted_iota(jnp.int32, (t, t), 1)
    neg_suffix = jnp.where(row > col, -1.0, 0.0).astype(BF16)
    causal = jnp.concatenate([col < row] * nh, axis=0)

    def tile(j, run, acc, diag):
        kt = k_ref[pl.ds(pl.multiple_of(j * t, t), t), :]
        z = _dot_nt(qs, kt)
        sp = _softplus2(z)
        spm = jnp.where(causal, sp, 0.0) if diag else sp
        between = _dot(spm.astype(BF16), neg_suffix) - run
        w = jnp.exp2(z - sp + between)
        if diag:
            w = jnp.where(causal, w, 0.0)
        wb = w.astype(BF16)
        w_heads = jnp.concatenate([wb[h * t:(h + 1) * t, :] for h in range(nh)], axis=1)
        acc = acc + _dot(w_heads, vm_ref[j])
        return run + jnp.sum(spm, axis=1, keepdims=True), acc

    run0, acc0 = jnp.zeros((nh * t, 1), F32), jnp.zeros((t, SB_LANES), F32)

    def diagonal_only():
        return (i - 1,) + tile(i, run0, acc0, True)

    def diagonal_and_previous():
        run, acc = tile(i, run0, acc0, True)
        return (i - 2,) + tile(i - 1, run, acc, False)

    j0, run, acc = lax.cond(i > 0, diagonal_and_previous, diagonal_only)

    def all_dead(run):
        return (jnp.min(run) > SB_DEAD_LOG2).astype(jnp.int32)

    def cond(c):
        return jnp.logical_and(c[0] >= 0, c[1] == 0)

    def body(c):
        j, _, run, acc = c
        run, acc = tile(j, run, acc, False)
        return j - 1, all_dead(run), run, acc

    _, _, run, acc = lax.while_loop(cond, body, (j0, all_dead(run), run, acc))
    o_ref[...] = acc.astype(o_ref.dtype)


def _sb_attention(qkv, batch, seq):
    n = qkv.shape[0]
    t = SB_TILE
    nt = seq // t
    ng = D_MODEL // SB_LANES
    nh = SB_LANES // SB_HEAD_DIM
    return pl.pallas_call(
        _sb_kernel,
        grid=(batch, ng, nt),
        in_specs=[
            pl.BlockSpec((t, SB_LANES), lambda b, g, i: (b * nt + i, g)),
            pl.BlockSpec((seq, SB_LANES), lambda b, g, i: (b, ng + g)),
            pl.BlockSpec((seq, SB_LANES), lambda b, g, i: (b, 2 * ng + g)),
        ],
        out_specs=pl.BlockSpec((t, SB_LANES), lambda b, g, i: (b * nt + i, g)),
        out_shape=jax.ShapeDtypeStruct((n, D_MODEL), BF16),
        scratch_shapes=[pltpu.VMEM((nt, nh * t, SB_LANES), BF16)],
        compiler_params=_params("arbitrary", "arbitrary", "arbitrary"),
        name="sb_attention",
    )(qkv, qkv, qkv)


def _router_kernel(x_ref, wr_ref, o_ref):
    logits = lax.dot_general(wr_ref[...], _load_row_tiles(x_ref), (((1,), (1,)), ((), ())),
                             precision=lax.Precision.HIGHEST,
                             preferred_element_type=F32)
    e_idx = lax.broadcasted_iota(jnp.int32, logits.shape, 0)
    m1 = jnp.max(logits, axis=0, keepdims=True)
    i1 = jnp.min(jnp.where(logits == m1, e_idx, N_EXPERTS), axis=0, keepdims=True)
    rest = jnp.where(e_idx == i1, -jnp.inf, logits)
    m2 = jnp.max(rest, axis=0, keepdims=True)
    i2 = jnp.min(jnp.where(rest == m2, e_idx, N_EXPERTS), axis=0, keepdims=True)
    e2 = jnp.exp(m2 - m1)
    den = 1.0 + e2
    g1 = 1.0 / den
    g2 = e2 / den
    zeros = jnp.zeros((N_EXPERTS - 4, logits.shape[1]), F32)
    o_ref[...] = jnp.concatenate([i1.astype(F32), i2.astype(F32), g1, g2, zeros], axis=0)


def _router(x, w_router_t, tm):
    n = x.shape[0] // NL
    return pl.pallas_call(
        _router_kernel,
        grid=(n // tm,),
        in_specs=[pl.BlockSpec((tm * NL, LANES), lambda i: (i, 0)),
                  pl.BlockSpec((N_EXPERTS, D_MODEL), lambda i: (0, 0))],
        out_specs=pl.BlockSpec((N_EXPERTS, tm), lambda i: (0, i)),
        out_shape=jax.ShapeDtypeStruct((N_EXPERTS, n), F32),
        compiler_params=_params("parallel"),
        name="router",
    )(x, w_router_t)


def _start_row_gather(idx_ref, src_ref, dst_ref, sem, base):
    def start(r, carry):
        pltpu.make_async_copy(src_ref.at[_token_rows(idx_ref[base + r])],
                              dst_ref.at[_token_rows(r)], sem).start()
        return carry

    lax.fori_loop(0, dst_ref.shape[0] // NL, start, 0, unroll=8)


def _wait_row_gather(src_ref, dst_ref, sem):
    pltpu.make_async_copy(src_ref.at[pl.ds(0, dst_ref.shape[0])], dst_ref, sem).wait()


def _dispatch_kernel(pos_ref, pad_start_ref, pad_count_ref, x_ref, out_ref, zero_ref, sems):
    i = pl.program_id(0)
    tm = x_ref.shape[0] // NL
    n = pl.num_programs(0) * tm
    fill_sem = sems.at[TOP_K]

    @pl.when(i == 0)
    def _():
        zero_ref[...] = jnp.zeros_like(zero_ref)
        for e in range(N_EXPERTS):
            def fill(r, carry, e=e):
                pltpu.make_async_copy(zero_ref, out_ref.at[_token_rows(pad_start_ref[e] + r)],
                                      fill_sem).start()
                return carry

            def drain(r, carry):
                pltpu.make_async_copy(zero_ref, out_ref.at[_token_rows(0)], fill_sem).wait()
                return carry

            lax.fori_loop(0, pad_count_ref[e], fill, 0)
            lax.fori_loop(0, pad_count_ref[e], drain, 0)

    for c in range(TOP_K):
        def start(r, carry, c=c):
            pltpu.make_async_copy(x_ref.at[_token_rows(r)],
                                  out_ref.at[_token_rows(pos_ref[c * n + i * tm + r])],
                                  sems.at[c]).start()
            return carry

        lax.fori_loop(0, tm, start, 0, unroll=8)
    for c in range(TOP_K):
        pltpu.make_async_copy(x_ref, out_ref.at[pl.ds(0, tm * NL)], sems.at[c]).wait()


def _dispatch_rows(x, pos, pad_start, pad_count, tokens):
    n = x.shape[0] // NL
    tm = GATHER_CHUNK
    grid_spec = pltpu.PrefetchScalarGridSpec(
        num_scalar_prefetch=3,
        grid=(n // tm,),
        in_specs=[pl.BlockSpec((tm * NL, LANES), lambda i, *_: (i, 0))],
        out_specs=pl.BlockSpec(memory_space=pl.ANY),
        scratch_shapes=[pltpu.VMEM((NL, LANES), x.dtype),
                        pltpu.SemaphoreType.DMA((TOP_K + 1,))],
    )
    return pl.pallas_call(
        _dispatch_kernel,
        grid_spec=grid_spec,
        out_shape=jax.ShapeDtypeStruct((tokens * NL, LANES), x.dtype),
        compiler_params=_params("arbitrary"),
        name="dispatch_rows",
    )(pos, pad_start, pad_count, x)


def _stream_cast(src_ref, dst_ref, stage_ref, sems):
    cr = stage_ref.shape[1]
    nchunks = src_ref.shape[0] // cr

    def copy(c):
        return pltpu.make_async_copy(src_ref.at[pl.ds(c * cr, cr)], stage_ref.at[c % 2],
                                     sems.at[c % 2])

    copy(0).start()
    for c in range(nchunks):
        if c + 1 < nchunks:
            copy(c + 1).start()
        copy(c).wait()
        dst_ref[c * cr:(c + 1) * cr, :] = stage_ref[c % 2].astype(dst_ref.dtype)


def _expert_kernel(te_ref, nu_ref, x_ref, wi_hbm, wo_hbm, o_ref,
                   wi_ref, wo_ref, stage_i, stage_o, sems):
    i = pl.program_id(0)
    used = i < nu_ref[0]
    e = te_ref[i]
    first_of_expert = jnp.logical_or(i == 0, e != te_ref[jnp.maximum(i - 1, 0)])

    @pl.when(jnp.logical_and(used, first_of_expert))
    def _():
        _stream_cast(wi_hbm.at[e], wi_ref, stage_i, sems.at[0])
        _stream_cast(wo_hbm.at[e], wo_ref, stage_o, sems.at[1])

    @pl.when(used)
    def _():
        xb = _load_row_tiles(x_ref).astype(BF16)
        y = None
        for c in range(D_EXPERT // MOE_TF):
            gate = _dot(xb, wi_ref[:, c * MOE_TF:(c + 1) * MOE_TF])
            up = _dot(xb, wi_ref[:, D_EXPERT + c * MOE_TF:D_EXPERT + (c + 1) * MOE_TF])
            part = _dot((_silu(gate) * up).astype(BF16), wo_ref[c * MOE_TF:(c + 1) * MOE_TF, :])
            y = part if y is None else y + part
        _store_row_tiles(o_ref, y)

    @pl.when(jnp.logical_not(used))
    def _():
        o_ref[...] = jnp.zeros_like(o_ref)


def _experts(xs, tile_expert, n_used, w_in, w_out):
    p = xs.shape[0] // NL
    d = D_MODEL
    tm = MOE_TM
    tiles = pl.BlockSpec((tm * NL, LANES), lambda i, te, nu: (i, 0))
    grid_spec = pltpu.PrefetchScalarGridSpec(
        num_scalar_prefetch=2,
        grid=(p // tm,),
        in_specs=[tiles, pl.BlockSpec(memory_space=pl.ANY), pl.BlockSpec(memory_space=pl.ANY)],
        out_specs=tiles,
        scratch_shapes=[pltpu.VMEM((d, 2 * D_EXPERT), BF16),
                        pltpu.VMEM((D_EXPERT, d), BF16),
                        pltpu.VMEM((2, MOE_STAGE_ROWS_IN, 2 * D_EXPERT), F32),
                        pltpu.VMEM((2, MOE_STAGE_ROWS_OUT, d), F32),
                        pltpu.SemaphoreType.DMA((2, 2))],
    )
    return pl.pallas_call(
        _expert_kernel,
        grid_spec=grid_spec,
        out_shape=jax.ShapeDtypeStruct((p * NL, LANES), F32),
        compiler_params=_params("arbitrary"),
        name="experts",
    )(tile_expert, n_used, xs, w_in, w_out)


def _combine_ln_kernel(pos_ref, x_ref, ys_ref, g1_ref, g2_ref, g_ref, b_ref, o_ref, ybuf, sems):
    i = pl.program_id(0)
    steps = pl.num_programs(0)
    tm = x_ref.shape[0] // NL
    n = steps * tm

    def start(step, slot):
        for c in range(TOP_K):
            _start_row_gather(pos_ref, ys_ref, ybuf.at[slot, c], sems.at[slot, c],
                              c * n + step * tm)

    @pl.when(i == 0)
    def _():
        start(0, 0)

    @pl.when(i + 1 < steps)
    def _():
        start(i + 1, (i + 1) % 2)

    slot = i % 2
    for c in range(TOP_K):
        _wait_row_gather(ys_ref, ybuf.at[slot, c], sems.at[slot, c])
    y1, y2 = _load_row_tiles(ybuf.at[slot, 0]), _load_row_tiles(ybuf.at[slot, 1])
    y = (DEEPNORM_ALPHA * _load_row_tiles(x_ref) + g1_ref[...] * y1) + g2_ref[...] * y2
    o_ref[...] = _layer_norm(y, g_ref[...], b_ref[...])


def _combine_ln(x, ys, pos, g1, g2, g, b, tm):
    n = x.shape[0] // NL
    d = D_MODEL
    grid_spec = pltpu.PrefetchScalarGridSpec(
        num_scalar_prefetch=1,
        grid=(n // tm,),
        in_specs=[pl.BlockSpec((tm * NL, LANES), lambda i, pos: (i, 0)),
                  pl.BlockSpec(memory_space=pl.ANY),
                  pl.BlockSpec((tm, 1), lambda i, pos: (i, 0)),
                  pl.BlockSpec((tm, 1), lambda i, pos: (i, 0)),
                  pl.BlockSpec((1, d), lambda i, pos: (0, 0)),
                  pl.BlockSpec((1, d), lambda i, pos: (0, 0))],
        out_specs=pl.BlockSpec((tm, d), lambda i, pos: (i, 0)),
        scratch_shapes=[pltpu.VMEM((2, TOP_K, tm * NL, LANES), F32),
                        pltpu.SemaphoreType.DMA((2, TOP_K))],
    )
    return pl.pallas_call(
        _combine_ln_kernel,
        grid_spec=grid_spec,
        out_shape=jax.ShapeDtypeStruct((n, d), F32),
        compiler_params=_params("arbitrary"),
        name="combine_ln",
    )(pos, x, ys, g1, g2, g, b)


def _moe(x, w_router, w_in, w_out, g, b):
    n = x.shape[0] // NL
    tm = MOE_TM
    info = _router(x, w_router.T, 1024)
    idx = info[:2].astype(jnp.int32).reshape(2 * n)
    gates = info[2:4]
    onehot = (idx[:, None] == jnp.arange(N_EXPERTS, dtype=jnp.int32)[None, :]).astype(jnp.int32)
    csum = jnp.cumsum(onehot, axis=0)
    rank = jnp.sum(csum * onehot, axis=1) - 1
    counts = csum[-1]
    padded = ((counts + tm - 1) // tm) * tm
    ends = jnp.cumsum(padded)
    starts = ends - padded
    pos = starts[idx] + rank
    p = 2 * n + N_EXPERTS * tm
    tile_start = jnp.arange(p // tm, dtype=jnp.int32) * tm
    tile_expert = jnp.minimum(
        jnp.sum((tile_start[:, None] >= ends[None, :]).astype(jnp.int32), axis=1), N_EXPERTS - 1)
    n_used = (ends[-1] // tm).astype(jnp.int32).reshape(1)
    pad_count = (padded - counts).at[N_EXPERTS - 1].add(p - ends[-1])
    xs = _dispatch_rows(x, pos, starts + counts, pad_count, p)
    ys = _experts(xs, tile_expert, n_used, w_in, w_out)
    return _combine_ln(x, ys, pos, gates[0][:, None], gates[1][:, None], g, b, 512)


def kernel(x, ret_w_in, ret_w_out, sb_w_kv, sb_w_q, sb_w_out, ln_mix_g, ln_mix_b,
           ln_ffn_g, ln_ffn_b, ffn_w_in, ffn_w_out, moe_w_router, moe_w_in, moe_w_out):
    batch, seq, d = x.shape
    n = batch * seq
    x0 = x.reshape(n, d)
    row = lambda v: v.reshape(1, d)

    ret_scale = jnp.concatenate([
        jnp.ones((1, D_MODEL), F32), jnp.full((1, D_MODEL), RET_QK_DIM ** -0.5, F32),
        jnp.ones((1, 2 * RET_V_WIDTH), F32)], axis=1)
    proj = _mm(x0, ret_w_in[0].astype(BF16), ret_scale, 1024, 512, BF16)
    og = _retention(proj, batch, seq)
    x1 = _mm_res_ln(og, ret_w_out[0].astype(BF16), x0, row(ln_mix_g[0]), row(ln_mix_b[0]), 512)
    x2 = _ffn(x1, ffn_w_in[0].astype(BF16), ffn_w_out[0].astype(BF16),
              row(ln_ffn_g[0]), row(ln_ffn_b[0]), 512)

    w_qkv = jnp.concatenate([sb_w_q[0], sb_w_kv], axis=1).astype(BF16)
    sb_scale = jnp.concatenate([
        jnp.full((1, D_MODEL), SB_HEAD_DIM ** -0.5 * LOG2_E, F32),
        jnp.ones((1, 2 * D_MODEL), F32)], axis=1)
    qkv = _mm(x2, w_qkv, sb_scale, 1024, 512, BF16)
    att = _sb_attention(qkv, batch, seq)
    x3 = _mm_res_ln(att, sb_w_out[0].astype(BF16), x2, row(ln_mix_g[1]), row(ln_mix_b[1]), 512,
                    row_tiles=True)
    x4 = _moe(x3, moe_w_router[0], moe_w_in[0], moe_w_out[0],
              row(ln_ffn_g[1]), row(ln_ffn_b[1]))
    return x4.reshape(batch, seq, d)
```
